```python
import functools
import jax
import jax.numpy as jnp
from jax import lax
import numpy as np


D_MODEL = 2048
BATCH = 1
SEQ = 8192
DEPTH = 1
DEC_BATCH = 32
DEC_SEQ = 1
PAST_LEN = 8192
PAGE_SIZE = 128

N_META = 16
D_MIX = D_MODEL
D_A = D_MIX // 2
DH_A = 128
H_A = D_A // DH_A
D_B = D_MIX - D_A
HS_B = 64
H_B = D_B // HS_B
W_LORA = 64
A_LORA = 64
G_LORA = 128
D_SHIFT = 3 * D_B + W_LORA + A_LORA + G_LORA
D_IN = 3 * D_A + D_SHIFT
SPLIT_B = [D_B, 2 * D_B, 3 * D_B, 3 * D_B + W_LORA, 3 * D_B + W_LORA + A_LORA]
Q_BLOCK = 128
N_KEYS = 128
N_EXPERTS = N_KEYS * N_KEYS
PEER_HEADS = 8
PEER_DQ = 256
PEER_TOPK = 16
ROW_BLOCK = 128
RMS_EPS = 1e-6
GN_EPS = 64e-5
L2_EPS = 1e-12

kernel_name = "hymba_stickbreak_rwkv7_peer_step"


def rmsnorm(x, g):
    xf = x.astype(jnp.float32)
    y = xf * lax.rsqrt(jnp.mean(xf * xf, axis=-1, keepdims=True) + RMS_EPS)
    return (y * g.astype(jnp.float32)).astype(x.dtype)


def sb_attend(q, k, v, bias, q_pos, k_pos):
    z = jnp.einsum('bqhd,bkhd->bhqk', q.astype(jnp.float32), k.astype(jnp.float32)) * (q.shape[-1] ** -0.5)
    z = z + bias.astype(jnp.float32)[None, :, None, None]
    mask = k_pos[None, :] < q_pos[:, None]
    log_keep = jnp.where(mask, jax.nn.log_sigmoid(-z), 0.0)
    tail = lax.cumsum(log_keep, axis=3, reverse=True) - log_keep
    w = jnp.where(mask, jnp.exp(jax.nn.log_sigmoid(z) + tail), 0.0)
    return jnp.einsum('bhqk,bkhd->bqhd', w, v.astype(jnp.float32)).astype(v.dtype)


def sb_prompt(q, k, v, bias):
    B, T, H, Dh = q.shape
    pad = (-T) % Q_BLOCK
    widths = ((0, 0), (0, pad), (0, 0), (0, 0))
    qp = jnp.pad(q, widths)
    kp = jnp.pad(k, widths)
    vp = jnp.pad(v, widths)
    Tp = T + pad
    nb = Tp // Q_BLOCK
    pos = jnp.arange(Tp, dtype=jnp.int32)
    q_blocks = jnp.moveaxis(qp.reshape(B, nb, Q_BLOCK, H, Dh), 1, 0)
    pos_blocks = pos.reshape(nb, Q_BLOCK)
    out = lax.map(lambda blk: sb_attend(blk[0], kp, vp, bias, blk[1], pos), (q_blocks, pos_blocks))
    return jnp.moveaxis(out, 0, 1).reshape(B, Tp, H, Dh)[:, :T]


def sb_sample(q, k, v, bias, cache_k_l, cache_v_l, page_table):
    Bd, S = q.shape[0], q.shape[1]
    past = page_table.shape[1] * PAGE_SIZE
    k_past = cache_k_l[page_table].reshape(Bd, past, H_A, DH_A).astype(k.dtype)
    v_past = cache_v_l[page_table].reshape(Bd, past, H_A, DH_A).astype(v.dtype)
    k_all = jnp.concatenate([k_past, k], axis=1)
    v_all = jnp.concatenate([v_past, v], axis=1)
    q_pos = past + jnp.arange(S, dtype=jnp.int32)
    k_pos = jnp.arange(past + S, dtype=jnp.int32)
    return sb_attend(q, k_all, v_all, bias, q_pos, k_pos)


def rwkv_time_mix(pb, shift0, wkv0, mu_shift, w0, w_up, a0, a_up, g_up, k_k, k_a, r_k, ln_x_g, ln_x_b):
    B, T, _ = pb.shape
    f32 = jnp.float32
    p = pb.astype(f32)
    prev = jnp.concatenate([shift0.astype(f32)[:, None], p[:, :-1]], axis=1)
    xs = p + mu_shift.astype(f32) * (prev - p)
    r, k, v, w_lo, a_lo, g_lo = jnp.split(xs, SPLIT_B, axis=-1)
    w = -jax.nn.softplus(-(w0.astype(f32) + jnp.tanh(w_lo) @ w_up.astype(f32))) - 0.5
    decay = jnp.exp(-jnp.exp(w))
    a = jax.nn.sigmoid(a0.astype(f32) + a_lo @ a_up.astype(f32))
    g = jax.nn.sigmoid(g_lo) @ g_up.astype(f32)
    heads = lambda t: t.reshape(B, T, H_B, HS_B)
    kk = heads(k * k_k.astype(f32))
    kk = kk * lax.rsqrt(jnp.sum(kk * kk, axis=-1, keepdims=True) + L2_EPS)
    k = k * (1.0 + (a - 1.0) * k_a.astype(f32))
    r, k, v, a, decay = heads(r), heads(k), heads(v), heads(a), heads(decay)

    def step(S, inp):
        r_t, k_t, v_t, kk_t, a_t, d_t = inp
        s_kk = jnp.einsum('bhvk,bhk->bhv', S, kk_t)
        S = (S * d_t[:, :, None, :]
             - s_kk[..., None] * (kk_t * a_t)[:, :, None, :]
             + v_t[..., None] * k_t[:, :, None, :])
        return S, jnp.einsum('bhvk,bhk->bhv', S, r_t)

    tm = lambda t: jnp.moveaxis(t, 1, 0)
    S_fin, y = lax.scan(step, wkv0.astype(f32), (tm(r), tm(k), tm(v), tm(kk), tm(a), tm(decay)))
    y = jnp.moveaxis(y, 0, 1)
    mu = jnp.mean(y, axis=-1, keepdims=True)
    var = jnp.mean(jnp.square(y - mu), axis=-1, keepdims=True)
    y = ((y - mu) * lax.rsqrt(var + GN_EPS)).reshape(B, T, D_B) * ln_x_g.astype(f32) + ln_x_b.astype(f32)
    bonus = jnp.sum(r * k * r_k.astype(f32), axis=-1, keepdims=True) * v
    y = (y + bonus.reshape(B, T, D_B)) * g
    return y.astype(pb.dtype), S_fin.astype(wkv0.dtype), pb[:, -1].astype(shift0.dtype)


def peer_rows(h, wq, subkeys, u, v):
    R = h.shape[0]
    q = (h @ wq).reshape(R, PEER_HEADS, 2, PEER_DQ // 2)
    s = jnp.einsum('rhcd,hcnd->rhcn', q, subkeys).astype(jnp.float32)
    s_top, i_top = lax.top_k(s, PEER_TOPK)
    cand = (s_top[:, :, 0, :, None] + s_top[:, :, 1, None, :]).reshape(R, PEER_HEADS, PEER_TOPK * PEER_TOPK)
    c_top, c_idx = lax.top_k(cand, PEER_TOPK)
    e1 = jnp.take_along_axis(i_top[:, :, 0], c_idx // PEER_TOPK, axis=-1)
    e2 = jnp.take_along_axis(i_top[:, :, 1], c_idx % PEER_TOPK, axis=-1)
    experts = e1 * N_KEYS + e2
    gate = jax.nn.softmax(c_top, axis=-1)
    act = jax.nn.gelu(jnp.einsum('rd,rhkd->rhk', h, u[experts]).astype(jnp.float32))
    coef = (gate * act).astype(h.dtype)
    return jnp.einsum('rhk,rhkd->rd', coef, v[experts])


def peer_ffn(h, wq, subkeys, u, v):
    M = h.shape[0]
    pad = (-M) % ROW_BLOCK
    blocks = jnp.pad(h, ((0, pad), (0, 0))).reshape(-1, ROW_BLOCK, D_MODEL)
    out = lax.map(lambda hb: peer_rows(hb, wq, subkeys, u, v), blocks)
    return out.reshape(-1, D_MODEL)[:M]


def layer_forward(x, attend, wkv0, shift0, norm_mix_g, w_in, q_norm_g, k_norm_g, beta_a, sb_bias, mu_shift,
                  w0, w_up, a0, a_up, g_up, k_k, k_a, r_k, ln_x_g, ln_x_b, w_out, norm_ffn_g, peer_wq,
                  peer_subkeys, peer_u, peer_v):
    B, T, _ = x.shape
    p = rmsnorm(x, norm_mix_g) @ w_in
    qa, ka, va, pb = jnp.split(p, [D_A, 2 * D_A, 3 * D_A], axis=-1)
    qa = rmsnorm(qa.reshape(B, T, H_A, DH_A), q_norm_g)
    ka = rmsnorm(ka.reshape(B, T, H_A, DH_A), k_norm_g)
    va = va.reshape(B, T, H_A, DH_A)
    o_a = attend(qa, ka, va, sb_bias).reshape(B, T, D_A) * beta_a
    y_b, wkv, shift = rwkv_time_mix(pb, shift0, wkv0, mu_shift, w0, w_up, a0, a_up, g_up, k_k, k_a, r_k,
                                    ln_x_g, ln_x_b)
    x = x + jnp.concatenate([o_a.astype(x.dtype), y_b.astype(x.dtype)], axis=-1) @ w_out
    h = rmsnorm(x, norm_ffn_g).reshape(B * T, D_MODEL)
    x = x + peer_ffn(h, peer_wq, peer_subkeys, peer_u, peer_v).reshape(B, T, D_MODEL)
    return x, ka, va, wkv, shift


def setup_inputs(seed: int = 0) -> dict:
    key = jax.random.key(seed)
    ks = jax.random.split(key, 32)
    f32 = jnp.float32
    L = DEPTH
    n_pages = PAST_LEN // PAGE_SIZE
    n_used = DEC_BATCH * n_pages
    n_phys = n_used + max(1, n_used // 4)

    def nrm(i, shape, scale=1.0):
        return jax.random.normal(ks[i], shape, f32) * scale

    def gain(i, shape):
        return 1.0 + nrm(i, shape, 0.02)

    page_table = jax.random.permutation(ks[4], n_phys)[:n_used].reshape(DEC_BATCH, n_pages).astype(jnp.int32)
    return {
        'x_prompt': nrm(0, (BATCH, SEQ, D_MODEL)),
        'x_sample': nrm(1, (DEC_BATCH, DEC_SEQ, D_MODEL)),
        'cache_k': nrm(2, (L, n_phys, PAGE_SIZE, H_A, DH_A)),
        'cache_v': nrm(3, (L, n_phys, PAGE_SIZE, H_A, DH_A)),
        'page_table': page_table,
        'state_wkv': nrm(5, (L, DEC_BATCH, H_B, HS_B, HS_B), 0.3),
        'state_shift': nrm(6, (L, DEC_BATCH, D_SHIFT)),
        'meta_tokens': nrm(7, (N_META, D_MODEL)),
        'norm_mix_g': gain(8, (L, D_MODEL)),
        'w_in': nrm(9, (L, D_MODEL, D_IN), D_MODEL ** -0.5),
        'q_norm_g': gain(10, (L, DH_A)),
        'k_norm_g': gain(11, (L, DH_A)),
        'beta_a': gain(12, (L, D_A)),
        'sb_bias': jax.random.uniform(ks[30], (L, H_A), f32, -10.0, -5.0),
        'mu_shift': jax.random.uniform(ks[13], (L, D_SHIFT), f32, 0.0, 1.0),
        'w0': jax.random.uniform(ks[14], (L, D_B), f32, -6.0, -1.0),
        'w_up': nrm(15, (L, W_LORA, D_B), 0.1 * W_LORA ** -0.5),
        'a0': nrm(16, (L, D_B), 0.1),
        'a_up': nrm(17, (L, A_LORA, D_B), A_LORA ** -0.5),
        'g_up': nrm(18, (L, G_LORA, D_B), G_LORA ** -0.5),
        'k_k': 0.85 + nrm(19, (L, D_B), 0.02),
        'k_a': gain(20, (L, D_B)),
        'r_k': nrm(21, (L, H_B, HS_B), 0.1),
        'ln_x_g': gain(22, (L, D_B)),
        'ln_x_b': nrm(23, (L, D_B), 0.02),
        'w_out': nrm(24, (L, D_MIX, D_MODEL), D_MIX ** -0.5),
        'norm_ffn_g': gain(25, (L, D_MODEL)),
        'peer_wq': nrm(26, (L, D_MODEL, PEER_HEADS * PEER_DQ), D_MODEL ** -0.5),
        'peer_subkeys': nrm(27, (L, PEER_HEADS, 2, N_KEYS, PEER_DQ // 2), (PEER_DQ // 2) ** -0.5),
        'peer_u': nrm(28, (L, N_EXPERTS, D_MODEL), D_MODEL ** -0.5),
        'peer_v': nrm(29, (L, N_EXPERTS, D_MODEL), PEER_HEADS ** -0.5),
    }


def reference(x_prompt, x_sample, cache_k, cache_v, page_table, state_wkv, state_shift, meta_tokens,
              norm_mix_g, w_in, q_norm_g, k_norm_g, beta_a, sb_bias, mu_shift, w0, w_up, a0, a_up, g_up, k_k,
              k_a, r_k, ln_x_g, ln_x_b, w_out, norm_ffn_g, peer_wq, peer_subkeys, peer_u, peer_v):
    B = x_prompt.shape[0]
    meta = jnp.broadcast_to(meta_tokens.astype(x_prompt.dtype)[None], (B, N_META, D_MODEL))
    xp = jnp.concatenate([meta, x_prompt], axis=1)
    xs = x_sample
    k_p, v_p, k_s, v_s, wkv_p, wkv_s, sh_p, sh_s = [], [], [], [], [], [], [], []
    for l in range(DEPTH):
        lp = (norm_mix_g[l], w_in[l], q_norm_g[l], k_norm_g[l], beta_a[l], sb_bias[l], mu_shift[l], w0[l],
              w_up[l], a0[l], a_up[l], g_up[l], k_k[l], k_a[l], r_k[l], ln_x_g[l], ln_x_b[l], w_out[l],
              norm_ffn_g[l], peer_wq[l], peer_subkeys[l], peer_u[l], peer_v[l])
        wkv_zero = jnp.zeros((B, H_B, HS_B, HS_B), state_wkv.dtype)
        shift_zero = jnp.zeros((B, D_SHIFT), state_shift.dtype)
        xp, kl, vl, wl, sl = layer_forward(xp, sb_prompt, wkv_zero, shift_zero, *lp)
        k_p.append(kl)
        v_p.append(vl)
        wkv_p.append(wl)
        sh_p.append(sl)
        attend_s = functools.partial(sb_sample, cache_k_l=cache_k[l], cache_v_l=cache_v[l], page_table=page_table)
        xs, kl, vl, wl, sl = layer_forward(xs, attend_s, state_wkv[l], state_shift[l], *lp)
        k_s.append(kl)
        v_s.append(vl)
        wkv_s.append(wl)
        sh_s.append(sl)
    y_prompt = xp[:, N_META:]
    return (y_prompt, xs, jnp.stack(k_p), jnp.stack(v_p), jnp.stack(k_s), jnp.stack(v_s),
            jnp.stack(wkv_p), jnp.stack(wkv_s), jnp.stack(sh_p), jnp.stack(sh_s))
```

```python
import functools

import jax
import jax.numpy as jnp
from jax import lax
from jax.experimental import pallas as pl
from jax.experimental.pallas import tpu as pltpu

F32 = jnp.float32
BF16 = jnp.bfloat16
I32 = jnp.int32
HIGHEST = lax.Precision.HIGHEST

LANES = 128
ROW_TILE = 256
VMEM_LIMIT = 56 * 1024 * 1024

N_META = 16
DH_A = 128
HS_B = 64
W_LORA = 64
A_LORA = 64
G_LORA = 128
N_KEYS = 128
PEER_HEADS = 8
PEER_TOPK = 16
PAGE_SIZE = 128
RMS_EPS = 1e-6
GN_EPS = 64e-5
L2_EPS = 1e-12
CHUNK = 64
SAMPLE_ROWS = 8
EXPERT_CHUNK = 1024


def _params(*sem):
    return pltpu.CompilerParams(dimension_semantics=sem, vmem_limit_bytes=VMEM_LIMIT)


def _bdot(a, b):
    return jnp.dot(a.astype(BF16), b.astype(BF16), preferred_element_type=F32)


def _bdot_nt(a, b):
    return lax.dot_general(a.astype(BF16), b.astype(BF16), (((1,), (1,)), ((), ())),
                           preferred_element_type=F32)


def _fdot(a, b):
    return jnp.dot(a, b, precision=HIGHEST, preferred_element_type=F32)


def _fdot_nt(a, b):
    return lax.dot_general(a, b, (((1,), (1,)), ((), ())), precision=HIGHEST, preferred_element_type=F32)


def _fdot_tn(a, b):
    return lax.dot_general(a, b, (((0,), (0,)), ((), ())), precision=HIGHEST, preferred_element_type=F32)


def _split2(x):
    hi = x.astype(BF16)
    lo = (x - hi.astype(F32)).astype(BF16)
    return hi, lo


def _rmsnorm_kernel(x_ref, g_ref, o_ref):
    x = x_ref[...]
    ms = jnp.mean(x * x, axis=-1, keepdims=True)
    o_ref[...] = (x * lax.rsqrt(ms + RMS_EPS) * g_ref[...]).astype(o_ref.dtype)


def rmsnorm_rows(x, g):
    m, d = x.shape
    return pl.pallas_call(
        _rmsnorm_kernel,
        grid=(m // ROW_TILE,),
        in_specs=[pl.BlockSpec((ROW_TILE, d), lambda i: (i, 0)),
                  pl.BlockSpec((1, d), lambda i: (0, 0))],
        out_specs=pl.BlockSpec((ROW_TILE, d), lambda i: (i, 0)),
        out_shape=jax.ShapeDtypeStruct((m, d), BF16),
        compiler_params=_params("parallel"),
        name="rmsnorm_rows",
    )(x, g.reshape(1, d))


def _mm_kernel(*refs, n_in, has_res):
    o_ref = refs[-1]
    acc = None
    for a_ref, w_ref in zip(refs[:n_in], refs[n_in:2 * n_in]):
        t = jnp.dot(a_ref[...].astype(BF16), w_ref[...], preferred_element_type=F32)
        acc = t if acc is None else acc + t
    if has_res:
        acc = acc + refs[2 * n_in][...]
    o_ref[...] = acc


def matmul_rows(a_list, w_list, res=None, *, tn):
    m = a_list[0].shape[0]
    n = w_list[0].shape[1]
    n_in = len(a_list)
    in_specs = [pl.BlockSpec((ROW_TILE, a.shape[1]), lambda j, i: (i, 0)) for a in a_list]
    in_specs += [pl.BlockSpec((w.shape[0], tn), lambda j, i: (0, j)) for w in w_list]
    args = list(a_list) + list(w_list)
    if res is not None:
        in_specs.append(pl.BlockSpec((ROW_TILE, tn), lambda j, i: (i, j)))
        args.append(res)
    return pl.pallas_call(
        functools.partial(_mm_kernel, n_in=n_in, has_res=res is not None),
        grid=(n // tn, m // ROW_TILE),
        in_specs=in_specs,
        out_specs=pl.BlockSpec((ROW_TILE, tn), lambda j, i: (i, j)),
        out_shape=jax.ShapeDtypeStruct((m, n), F32),
        compiler_params=_params("parallel", "parallel"),
        name="matmul_rows",
    )(*args)


def _qknorm_kernel(p_ref, qg_ref, kg_ref, q_ref, k_ref, *, n_heads):
    for h in range(n_heads):
        for src, g_ref, dst in ((h, qg_ref, q_ref), (n_heads + h, kg_ref, k_ref)):
            x = p_ref[:, src * DH_A:(src + 1) * DH_A]
            ms = jnp.mean(x * x, axis=-1, keepdims=True)
            dst[:, h * DH_A:(h + 1) * DH_A] = x * lax.rsqrt(ms + RMS_EPS) * g_ref[...]


def qk_norm(p, q_g, k_g, d_a):
    m = p.shape[0]
    n_heads = d_a // DH_A
    return pl.pallas_call(
        functools.partial(_qknorm_kernel, n_heads=n_heads),
        grid=(m // ROW_TILE,),
        in_specs=[pl.BlockSpec((ROW_TILE, 2 * d_a), lambda i: (i, 0)),
                  pl.BlockSpec((1, DH_A), lambda i: (0, 0)),
                  pl.BlockSpec((1, DH_A), lambda i: (0, 0))],
        out_specs=[pl.BlockSpec((ROW_TILE, d_a), lambda i: (i, 0)),
                   pl.BlockSpec((ROW_TILE, d_a), lambda i: (i, 0))],
        out_shape=[jax.ShapeDtypeStruct((m, d_a), F32), jax.ShapeDtypeStruct((m, d_a), F32)],
        compiler_params=_params("parallel"),
        name="qk_norm",
    )(p, q_g.reshape(1, DH_A), k_g.reshape(1, DH_A))


def _log_sigmoid(z):
    return jnp.minimum(z, 0.0) - jnp.log1p(jnp.exp(-jnp.abs(z)))


def _strict_after(n):
    r = lax.broadcasted_iota(I32, (n, n), 0)
    c = lax.broadcasted_iota(I32, (n, n), 1)
    return (r > c).astype(BF16)


def _sb_block(z, after, run, mask):
    ls = _log_sigmoid(z)
    lk = ls - z
    if mask is not None:
        lk = jnp.where(mask, lk, 0.0)
    hi, lo = _split2(lk)
    tail = (jnp.dot(hi, after, preferred_element_type=F32)
            + jnp.dot(lo, after, preferred_element_type=F32))
    w = jnp.exp(ls + tail + run)
    if mask is not None:
        w = jnp.where(mask, w, 0.0)
    return w, jnp.sum(lk, axis=1, keepdims=True)


def _attn_kernel(bias_ref, q_ref, k_ref, v_ref, beta_ref, o_ref, *, blk):
    h = pl.program_id(0)
    i = pl.program_id(1)
    q = q_ref[...].astype(BF16)
    bias = bias_ref[h]
    scale = DH_A ** -0.5
    after = _strict_after(blk)
    q_pos = i * blk + lax.broadcasted_iota(I32, (blk, 1), 0)

    def body(jj, carry):
        acc, run = carry
        j = i - jj
        start = pl.multiple_of(j * blk, blk)
        kb = k_ref[pl.ds(start, blk), :].astype(BF16)
        vb = v_ref[pl.ds(start, blk), :].astype(BF16)
        z = lax.dot_general(q, kb, (((1,), (1,)), ((), ())), preferred_element_type=F32) * scale + bias
        k_pos = j * blk + lax.broadcasted_iota(I32, (1, blk), 1)
        w, lk_sum = _sb_block(z, after, run, k_pos < q_pos)
        acc = acc + jnp.dot(w.astype(BF16), vb, preferred_element_type=F32)
        return acc, run + lk_sum

    acc, _ = lax.fori_loop(0, i + 1, body,
                           (jnp.zeros((blk, DH_A), F32), jnp.zeros((blk, 1), F32)))
    o_ref[...] = acc * beta_ref[...]


def sb_attention_prompt(q, k, v, bias, beta):
    t, d_a = q.shape
    n_heads = d_a // DH_A
    blk = ROW_TILE
    return pl.pallas_call(
        functools.partial(_attn_kernel, blk=blk),
        grid=(n_heads, t // blk),
        in_specs=[pl.BlockSpec(memory_space=pltpu.SMEM),
                  pl.BlockSpec((blk, DH_A), lambda h, i: (i, h)),
                  pl.BlockSpec((t, DH_A), lambda h, i: (0, h)),
                  pl.BlockSpec((t, DH_A), lambda h, i: (0, h)),
                  pl.BlockSpec((1, DH_A), lambda h, i: (0, h))],
        out_specs=pl.BlockSpec((blk, DH_A), lambda h, i: (i, h)),
        out_shape=jax.ShapeDtypeStruct((t, d_a), F32),
        compiler_params=_params("parallel", "parallel"),
        name="sb_attention_prompt",
    )(bias, q, k, v, beta.reshape(1, d_a))


def _attn_sample_kernel(pt_ref, q_ref, bias_ref, beta_ref, k_ref, v_ref, o_ref, acc_ref, run_ref,
                        *, n_heads):
    j = pl.program_id(1)
    d_a = n_heads * DH_A
    head_of_lane = lax.broadcasted_iota(I32, (n_heads, d_a), 1) // DH_A
    own = head_of_lane == lax.broadcasted_iota(I32, (n_heads, d_a), 0)

    @pl.when(j == 0)
    def _():
        acc_ref[...] = jnp.zeros_like(acc_ref)
        run_ref[...] = jnp.zeros_like(run_ref)

    q_rows = jnp.where(own, jnp.broadcast_to(q_ref[0], (n_heads, d_a)), 0.0)
    z = _bdot_nt(q_rows, k_ref[0]) * (DH_A ** -0.5) + bias_ref[...]
    w, lk_sum = _sb_block(z, _strict_after(PAGE_SIZE), run_ref[...], None)
    acc_ref[...] += _bdot(w, v_ref[0])
    run_ref[...] += lk_sum

    @pl.when(j == pl.num_programs(1) - 1)
    def _():
        o = jnp.sum(jnp.where(own, acc_ref[...], 0.0), axis=0, keepdims=True)
        o_ref[0] = o * beta_ref[...]


def sb_attention_sample(q, cache_k, cache_v, page_table, bias, beta):
    b, d_a = q.shape
    n_heads = d_a // DH_A
    n_pages = page_table.shape[1]

    def page_map(bi, j, pt):
        return (pt[bi * n_pages + (n_pages - 1 - j)], 0, 0)

    grid_spec = pltpu.PrefetchScalarGridSpec(
        num_scalar_prefetch=1,
        grid=(b, n_pages),
        in_specs=[pl.BlockSpec((1, 1, d_a), lambda bi, j, pt: (bi, 0, 0)),
                  pl.BlockSpec((n_heads, 1), lambda bi, j, pt: (0, 0)),
                  pl.BlockSpec((1, d_a), lambda bi, j, pt: (0, 0)),
                  pl.BlockSpec((1, PAGE_SIZE, d_a), page_map),
                  pl.BlockSpec((1, PAGE_SIZE, d_a), page_map)],
        out_specs=pl.BlockSpec((1, 1, d_a), lambda bi, j, pt: (bi, 0, 0)),
        scratch_shapes=[pltpu.VMEM((n_heads, d_a), F32), pltpu.VMEM((n_heads, 1), F32)],
    )
    out = pl.pallas_call(
        functools.partial(_attn_sample_kernel, n_heads=n_heads),
        grid_spec=grid_spec,
        out_shape=jax.ShapeDtypeStruct((b, 1, d_a), F32),
        compiler_params=_params("parallel", "arbitrary"),
        name="sb_attention_sample",
    )(page_table.reshape(-1), q.reshape(b, 1, d_a), bias.reshape(n_heads, 1), beta.reshape(1, d_a),
      cache_k, cache_v)
    return out.reshape(b, d_a)


def _seg_sum(x, seg):
    hi, lo = _split2(x)
    return jnp.dot(hi, seg, preferred_element_type=F32) + jnp.dot(lo, seg, preferred_element_type=F32)


def _seg_bcast(y, seg_t):
    hi = y.astype(BF16)
    r1 = y - hi.astype(F32)
    mid = r1.astype(BF16)
    lo = (r1 - mid.astype(F32)).astype(BF16)
    return (jnp.dot(hi, seg_t, preferred_element_type=F32)
            + jnp.dot(mid, seg_t, preferred_element_type=F32)
            + jnp.dot(lo, seg_t, preferred_element_type=F32))


def _rwkv_prep_kernel(p_ref, prev_ref, valid_ref, mu_ref, w0_ref, a0_ref, kk_ref, ka_ref,
                      wup_ref, aup_ref, gup_ref, seg_ref, segt_ref,
                      r_ref, k_ref, v_ref, kn_ref, b_ref, ld_ref, g_ref, *, d_b):
    p = p_ref[...]
    xs = p + mu_ref[...] * (prev_ref[...] - p)
    valid = valid_ref[...]
    r = xs[:, :d_b]
    k = xs[:, d_b:2 * d_b]
    v = xs[:, 2 * d_b:3 * d_b]
    o = 3 * d_b
    w_lo = xs[:, o:o + W_LORA]
    a_lo = xs[:, o + W_LORA:o + W_LORA + A_LORA]
    g_lo = xs[:, o + W_LORA + A_LORA:o + W_LORA + A_LORA + G_LORA]
    w = -jax.nn.softplus(-(w0_ref[...] + _bdot(jnp.tanh(w_lo), wup_ref[...]))) - 0.5
    log_decay = -jnp.exp(w)
    a = jax.nn.sigmoid(a0_ref[...] + _bdot(a_lo, aup_ref[...]))
    g = _bdot(jax.nn.sigmoid(g_lo), gup_ref[...])
    kk = k * kk_ref[...]
    inv = lax.rsqrt(_seg_sum(kk * kk, seg_ref[...]) + L2_EPS)
    kk = kk * _seg_bcast(inv, segt_ref[...])
    k2 = k * (1.0 + (a - 1.0) * ka_ref[...])
    r_ref[...] = r
    k_ref[...] = k2 * valid
    v_ref[...] = v * valid
    kn_ref[...] = kk * valid
    b_ref[...] = kk * a * valid
    ld_ref[...] = log_decay * valid
    g_ref[...] = g


def rwkv_prep(pb, prev, valid, lw):
    m, d_shift = pb.shape
    d_b = lw["w0"].shape[0]
    n_heads = d_b // HS_B
    assert n_heads <= LANES
    seg = (jnp.arange(d_b)[:, None] // HS_B == jnp.arange(LANES)[None, :]).astype(BF16)
    row = lambda n: pl.BlockSpec((ROW_TILE, n), lambda i: (i, 0))
    full = lambda a, b: pl.BlockSpec((a, b), lambda i: (0, 0))
    return pl.pallas_call(
        functools.partial(_rwkv_prep_kernel, d_b=d_b),
        grid=(m // ROW_TILE,),
        in_specs=[row(d_shift), row(d_shift), row(1), full(1, d_shift), full(1, d_b), full(1, d_b),
                  full(1, d_b), full(1, d_b), full(W_LORA, d_b), full(A_LORA, d_b), full(G_LORA, d_b),
                  full(d_b, LANES), full(LANES, d_b)],
        out_specs=[row(d_b)] * 7,
        out_shape=[jax.ShapeDtypeStruct((m, d_b), F32)] * 7,
        compiler_params=_params("parallel"),
        name="rwkv_prep",
    )(pb, prev, valid, lw["mu_shift"].reshape(1, -1), lw["w0"].reshape(1, -1), lw["a0"].reshape(1, -1),
      lw["k_k"].reshape(1, -1), lw["k_a"].reshape(1, -1), lw["w_up"].astype(BF16), lw["a_up"].astype(BF16),
      lw["g_up"].astype(BF16), seg, seg.T)


def _rwkv_chunk(state, r, k, v, kk, beta, ld, incl, strict):
    c = r.shape[0]
    cum = _fdot(incl, ld)
    e_in = jnp.exp(cum)
    e_ex = jnp.exp(cum - ld)
    e_neg = jnp.exp(-cum)
    left = jnp.concatenate([kk * e_ex, r * e_in], axis=0)
    right = jnp.concatenate([k * e_neg, beta * e_neg], axis=0)
    gram = _fdot_nt(left, right)
    from_state = _fdot_nt(left, state)
    a_k = gram[:c, :c] * strict
    a_b = gram[:c, c:] * strict
    m_k = gram[c:, :c] * incl
    m_b = gram[c:, c:] * incl
    u = from_state[:c] + _fdot(a_k, v)
    pw = -a_b
    n = 1
    while True:
        u = u + _fdot(pw, u)
        n *= 2
        if n >= c:
            break
        pw = _fdot(pw, pw)
    vu = jnp.concatenate([v, -u], axis=0)
    y = from_state[c:] + _fdot(jnp.concatenate([m_k, m_b], axis=1), vu)
    new_state = (state + _fdot_tn(vu, right)) * e_in[c - 1:c, :]
    return new_state, y


def _rwkv_scan_kernel(r_ref, k_ref, v_ref, kk_ref, b_ref, ld_ref, g_ref, lng_ref, lnb_ref, rk_ref,
                      s0_ref, y_ref, s_ref, state_ref, *, chunk, n_chunks):
    i = pl.program_id(2)

    @pl.when(i == 0)
    def _():
        state_ref[...] = s0_ref[0]

    row = lax.broadcasted_iota(I32, (chunk, chunk), 0)
    col = lax.broadcasted_iota(I32, (chunk, chunk), 1)
    incl = (col <= row).astype(F32)
    strict = (col < row).astype(F32)

    def body(ci, carry):
        rows = pl.ds(pl.multiple_of(ci * chunk, chunk), chunk)
        for hh in range(LANES // HS_B):
            lanes = slice(hh * HS_B, (hh + 1) * HS_B)
            r = r_ref[rows, lanes]
            k = k_ref[rows, lanes]
            v = v_ref[rows, lanes]
            new_state, y = _rwkv_chunk(state_ref[hh], r, k, v, kk_ref[rows, lanes], b_ref[rows, lanes],
                                       ld_ref[rows, lanes], incl, strict)
            state_ref[hh] = new_state
            mu = jnp.mean(y, axis=-1, keepdims=True)
            var = jnp.mean(jnp.square(y - mu), axis=-1, keepdims=True)
            yn = (y - mu) * lax.rsqrt(var + GN_EPS) * lng_ref[:, lanes] + lnb_ref[:, lanes]
            bonus = jnp.sum(r * k * rk_ref[:, lanes], axis=-1, keepdims=True) * v
            y_ref[rows, lanes] = (yn + bonus) * g_ref[rows, lanes]
        return carry

    lax.fori_loop(0, n_chunks, body, 0)

    @pl.when(i == pl.num_programs(2) - 1)
    def _():
        s_ref[0] = state_ref[...]


def rwkv_scan(prep, s0, ln_g, ln_b, r_k, *, batch, chunk, n_chunks):
    r, k, v, kk, beta, ld, g = prep
    m, d_b = r.shape
    t = m // batch
    blk = chunk * n_chunks
    n_blk = t // blk
    pairs = d_b // LANES
    hp = LANES // HS_B
    seq = pl.BlockSpec((blk, LANES), lambda b, p, i: (b * n_blk + i, p))
    par = pl.BlockSpec((1, LANES), lambda b, p, i: (0, p))
    st = pl.BlockSpec((1, hp, HS_B, HS_B), lambda b, p, i: (b, p, 0, 0))
    return pl.pallas_call(
        functools.partial(_rwkv_scan_kernel, chunk=chunk, n_chunks=n_chunks),
        grid=(batch, pairs, n_blk),
        in_specs=[seq] * 7 + [par] * 3 + [st],
        out_specs=[seq, st],
        out_shape=[jax.ShapeDtypeStruct((m, d_b), F32), jax.ShapeDtypeStruct(s0.shape, F32)],
        scratch_shapes=[pltpu.VMEM((hp, HS_B, HS_B), F32)],
        compiler_params=_params("parallel", "parallel", "arbitrary"),
        name="rwkv_scan",
    )(r, k, v, kk, beta, ld, g, ln_g.reshape(1, d_b), ln_b.reshape(1, d_b), r_k.reshape(1, d_b), s0)


def _topk_rows(s, n_out, neg):
    n = s.shape[0]
    rows = lax.broadcasted_iota(I32, s.shape, 0)
    vals, idxs = [], []
    for _ in range(n_out):
        m = jnp.max(s, axis=0, keepdims=True)
        idx = jnp.min(jnp.where(s == m, rows, n), axis=0, keepdims=True)
        s = jnp.where(rows == idx, neg, s)
        vals.append(m)
        idxs.append(idx)
    return vals, idxs


def _peer_topk_kernel(q_ref, keys_ref, e1_ref, e2_ref, gate_ref):
    neg = -jnp.inf
    dq = keys_ref.shape[-1]
    e1_rows, e2_rows, gate_rows = [], [], []
    for h in range(PEER_HEADS):
        tops = []
        for c in range(2):
            qc = q_ref[:, (2 * h + c) * dq:(2 * h + c + 1) * dq]
            s_t = _bdot_nt(keys_ref[h, c], qc)
            tops.append(_topk_rows(s_t, PEER_TOPK, neg))
        (v0, i0), (v1, i1) = tops
        s1 = jnp.concatenate(v1, axis=0)
        cand = jnp.concatenate([v0[a] + s1 for a in range(PEER_TOPK)], axis=0)
        c_val, c_idx = _topk_rows(cand, PEER_TOPK, neg)
        c_max = c_val[0]
        ex = [jnp.exp(cv - c_max) for cv in c_val]
        den = ex[0]
        for e in ex[1:]:
            den = den + e
        for kk in range(PEER_TOPK):
            a_sel = c_idx[kk] // PEER_TOPK
            b_sel = c_idx[kk] % PEER_TOPK
            e1 = jnp.zeros_like(a_sel)
            e2 = jnp.zeros_like(a_sel)
            for a in range(PEER_TOPK):
                e1 = jnp.where(a_sel == a, i0[a], e1)
                e2 = jnp.where(b_sel == a, i1[a], e2)
            e1_rows.append(e1)
            e2_rows.append(e2)
            gate_rows.append(ex[kk] / den)
    e1_ref[...] = jnp.concatenate(e1_rows, axis=0).T
    e2_ref[...] = jnp.concatenate(e2_rows, axis=0).T
    gate_ref[...] = jnp.concatenate(gate_rows, axis=0).T


def peer_topk(q, subkeys):
    m = q.shape[0]
    n_sel = PEER_HEADS * PEER_TOPK
    out = pl.BlockSpec((ROW_TILE, n_sel), lambda i: (i, 0))
    return pl.pallas_call(
        _peer_topk_kernel,
        grid=(m // ROW_TILE,),
        in_specs=[pl.BlockSpec((ROW_TILE, q.shape[1]), lambda i: (i, 0)),
                  pl.BlockSpec(subkeys.shape, lambda i: (0, 0, 0, 0))],
        out_specs=[out, out, out],
        out_shape=[jax.ShapeDtypeStruct((m, n_sel), I32), jax.ShapeDtypeStruct((m, n_sel), I32),
                   jax.ShapeDtypeStruct((m, n_sel), F32)],
        compiler_params=_params("parallel"),
        name="peer_topk",
    )(q, subkeys)


def _peer_act_kernel(h_ref, u_ref, e1_ref, e2_ref, gate_ref, coef_ref, pre_ref):
    s = pl.program_id(1)

    @pl.when(s == 0)
    def _():
        pre_ref[...] = jnp.zeros_like(pre_ref)

    scores = lax.dot_general(h_ref[...], u_ref[...], (((1,), (1,)), ((), ())),
                             preferred_element_type=F32)
    e1 = e1_ref[...]
    e2 = e2_ref[...]
    pre = pre_ref[...]
    for t in range(EXPERT_CHUNK // N_KEYS):
        picked = jnp.take_along_axis(scores[:, t * N_KEYS:(t + 1) * N_KEYS], e2, axis=1)
        pre = jnp.where(e1 == s * (EXPERT_CHUNK // N_KEYS) + t, picked, pre)
    pre_ref[...] = pre

    @pl.when(s == pl.num_programs(1) - 1)
    def _():
        coef_ref[...] = gate_ref[...] * jax.nn.gelu(pre)


def peer_act(hn, u, e1, e2, gate):
    m, d = hn.shape
    n_sel = e1.shape[1]
    sel = pl.BlockSpec((ROW_TILE, n_sel), lambda i, s: (i, 0))
    return pl.pallas_call(
        _peer_act_kernel,
        grid=(m // ROW_TILE, u.shape[0] // EXPERT_CHUNK),
        in_specs=[pl.BlockSpec((ROW_TILE, d), lambda i, s: (i, 0)),
                  pl.BlockSpec((EXPERT_CHUNK, d), lambda i, s: (s, 0)),
                  sel, sel, sel],
        out_specs=sel,
        out_shape=jax.ShapeDtypeStruct((m, n_sel), F32),
        scratch_shapes=[pltpu.VMEM((ROW_TILE, n_sel), F32)],
        compiler_params=_params("parallel", "arbitrary"),
        name="peer_act",
    )(hn, u, e1, e2, gate)


def _peer_mix_kernel(x_ref, v_ref, e1_ref, e2_ref, coef_ref, o_ref, w_ref):
    s = pl.program_id(1)
    per_step = EXPERT_CHUNK // N_KEYS

    @pl.when(s == 0)
    def _():
        o_ref[...] = x_ref[...]
        n_sel = e1_ref.shape[1]
        key_iota = lax.broadcasted_iota(I32, (N_KEYS, n_sel), 0)

        def row_body(r, carry):
            e1 = e1_ref[pl.ds(r, 1), :]
            e2 = e2_ref[pl.ds(r, 1), :]
            cf = coef_ref[pl.ds(r, 1), :]
            a_t = jnp.where(key_iota == e1, cf, 0.0).astype(BF16)
            b_t = (key_iota == e2).astype(BF16)
            w_ref[pl.ds(pl.multiple_of(r * N_KEYS, N_KEYS), N_KEYS), :] = lax.dot_general(
                a_t, b_t, (((1,), (1,)), ((), ())), preferred_element_type=F32)
            return carry

        lax.fori_loop(0, x_ref.shape[0], row_body, 0)

    rows = x_ref.shape[0]
    tiles = [w_ref[pl.ds(s * per_step + t, rows, stride=N_KEYS), :].astype(BF16) for t in range(per_step)]
    o_ref[...] += jnp.dot(jnp.concatenate(tiles, axis=1), v_ref[...], preferred_element_type=F32)


def peer_mix(x, v, e1, e2, coef):
    m, d = x.shape
    n_sel = e1.shape[1]
    sel = pl.BlockSpec((ROW_TILE, n_sel), lambda i, s: (i, 0))
    return pl.pallas_call(
        _peer_mix_kernel,
        grid=(m // ROW_TILE, v.shape[0] // EXPERT_CHUNK),
        in_specs=[pl.BlockSpec((ROW_TILE, d), lambda i, s: (i, 0)),
                  pl.BlockSpec((EXPERT_CHUNK, d), lambda i, s: (s, 0)),
                  sel, sel, sel],
        out_specs=pl.BlockSpec((ROW_TILE, d), lambda i, s: (i, 0)),
        out_shape=jax.ShapeDtypeStruct((m, d), F32),
        scratch_shapes=[pltpu.VMEM((ROW_TILE * N_KEYS, N_KEYS), F32)],
        compiler_params=_params("parallel", "arbitrary"),
        name="peer_mix",
    )(x, v, e1, e2, coef)


def _layer(x_all, n_prompt, n_sample, cache_k, cache_v, page_table, wkv_s0, shift_s0, lw):
    m, d_model = x_all.shape
    d_a = lw["beta_a"].shape[0]
    d_b = lw["w0"].shape[0]
    n_heads_b = d_b // HS_B
    d_shift = lw["mu_shift"].shape[0]
    sample = slice(n_prompt, n_prompt + n_sample)

    hn = rmsnorm_rows(x_all, lw["norm_mix_g"])
    p = matmul_rows([hn], [lw["w_in"].astype(BF16)], tn=1280)
    qn, kn = qk_norm(p, lw["q_norm_g"], lw["k_norm_g"], d_a)
    va = p[:, 2 * d_a:3 * d_a]
    pb = p[:, 3 * d_a:]

    o_a = sb_attention_prompt(qn, kn, va, lw["sb_bias"], lw["beta_a"])
    o_s = sb_attention_sample(qn[sample], cache_k, cache_v, page_table, lw["sb_bias"], lw["beta_a"])
    o_a = lax.dynamic_update_slice(o_a, o_s, (n_prompt, 0))

    prev = jnp.concatenate([jnp.zeros((1, d_shift), F32), pb[:-1]], axis=0)
    valid = (jnp.arange(m) < n_prompt).astype(F32)[:, None]
    prep = rwkv_prep(pb, prev, valid, lw)
    n_chunks = ROW_TILE // CHUNK
    y_b, wkv_p = rwkv_scan(prep, jnp.zeros((1, n_heads_b, HS_B, HS_B), F32), lw["ln_x_g"], lw["ln_x_b"],
                           lw["r_k"], batch=1, chunk=CHUNK, n_chunks=n_chunks)

    rows_s = n_sample * SAMPLE_ROWS
    rows_sp = -(-rows_s // ROW_TILE) * ROW_TILE
    spread = lambda a: jnp.pad(jnp.pad(a[:, None], ((0, 0), (0, SAMPLE_ROWS - 1), (0, 0))).reshape(rows_s, -1),
                               ((0, rows_sp - rows_s), (0, 0)))
    prep_s = rwkv_prep(spread(pb[sample]), spread(shift_s0), spread(jnp.ones((n_sample, 1), F32)), lw)
    y_s, wkv_s = rwkv_scan([a[:rows_s] for a in prep_s], wkv_s0, lw["ln_x_g"], lw["ln_x_b"], lw["r_k"],
                           batch=n_sample, chunk=SAMPLE_ROWS, n_chunks=1)
    y_b = lax.dynamic_update_slice(y_b, y_s[::SAMPLE_ROWS], (n_prompt, 0))

    w_out = lw["w_out"].astype(BF16)
    x1 = matmul_rows([o_a, y_b], [w_out[:d_a], w_out[d_a:]], res=x_all, tn=1024)

    hn2 = rmsnorm_rows(x1, lw["norm_ffn_g"])
    q = matmul_rows([hn2], [lw["peer_wq"].astype(BF16)], tn=1024)
    e1, e2, gate = peer_topk(q, lw["peer_subkeys"])
    coef = peer_act(hn2, lw["peer_u"].astype(BF16), e1, e2, gate)
    x2 = peer_mix(x1, lw["peer_v"].astype(BF16), e1, e2, coef)
    return x2, kn, va, wkv_p, wkv_s, pb


_LAYER_WEIGHTS = ("norm_mix_g", "w_in", "q_norm_g", "k_norm_g", "beta_a", "sb_bias", "mu_shift", "w0", "w_up",
                  "a0", "a_up", "g_up", "k_k", "k_a", "r_k", "ln_x_g", "ln_x_b", "w_out", "norm_ffn_g",
                  "peer_wq", "peer_subkeys", "peer_u", "peer_v")


def kernel(x_prompt, x_sample, cache_k, cache_v, page_table, state_wkv, state_shift, meta_tokens, norm_mix_g, w_in, q_norm_g, k_norm_g, beta_a, sb_bias, mu_shift, w0, w_up, a0, a_up, g_up, k_k, k_a, r_k, ln_x_g, ln_x_b, w_out, norm_ffn_g, peer_wq, peer_subkeys, peer_u, peer_v):
    weights = dict(zip(_LAYER_WEIGHTS, (norm_mix_g, w_in, q_norm_g, k_norm_g, beta_a, sb_bias, mu_shift, w0, w_up,
                                        a0, a_up, g_up, k_k, k_a, r_k, ln_x_g, ln_x_b, w_out, norm_ffn_g,
                                        peer_wq, peer_subkeys, peer_u, peer_v)))
    batch, seq, d_model = x_prompt.shape
    assert batch == 1 and x_sample.shape[1] == 1
    n_sample = x_sample.shape[0]
    n_prompt = seq + N_META
    depth = w_in.shape[0]
    m = -(-(n_prompt + n_sample) // ROW_TILE) * ROW_TILE
    x_all = jnp.concatenate([meta_tokens.astype(F32), x_prompt[0], x_sample[:, 0],
                             jnp.zeros((m - n_prompt - n_sample, d_model), F32)], axis=0)
    n_phys = cache_k.shape[1]
    d_a = beta_a.shape[1]
    n_heads_a = d_a // DH_A
    outs = [[] for _ in range(8)]
    for l in range(depth):
        lw = {name: w[l] for name, w in weights.items()}
        x_all, kn, va, wkv_p, wkv_s, pb = _layer(
            x_all, n_prompt, n_sample, cache_k[l].reshape(n_phys, PAGE_SIZE, d_a),
            cache_v[l].reshape(n_phys, PAGE_SIZE, d_a), page_table, state_wkv[l], state_shift[l], lw)
        heads = lambda a, rows: a[rows].reshape(-1, n_heads_a, DH_A)
        prompt = slice(0, n_prompt)
        sample = slice(n_prompt, n_prompt + n_sample)
        outs[0].append(heads(kn, prompt)[None])
        outs[1].append(heads(va, prompt)[None])
        outs[2].append(heads(kn, sample)[:, None])
        outs[3].append(heads(va, sample)[:, None])
        outs[4].append(wkv_p)
        outs[5].append(wkv_s)
        outs[6].append(pb[n_prompt - 1:n_prompt])
        outs[7].append(pb[sample])
    y_prompt = x_all[N_META:n_prompt][None]
    y_sample = x_all[n_prompt:n_prompt + n_sample][:, None]
    return (y_prompt, y_sample) + tuple(jnp.stack(o) for o in outs)
```

```python
import functools

import jax
import jax.numpy as jnp
from jax import lax
from jax.experimental import pallas as pl
from jax.experimental.pallas import tpu as pltpu

F32 = jnp.float32
BF16 = jnp.bfloat16
I32 = jnp.int32

LANES = 128
ROW_TILE = 256
VMEM_LIMIT = 56 * 1024 * 1024

N_META = 16
DH_A = 128
HS_B = 64
W_LORA = 64
A_LORA = 64
G_LORA = 128
N_KEYS = 128
PEER_HEADS = 8
PEER_TOPK = 16
PAGE_SIZE = 128
RMS_EPS = 1e-6
GN_EPS = 64e-5
L2_EPS = 1e-12
EXPERT_CHUNK = 1024
LOG2_E = 1.4426950408889634


def _params(*sem):
    return pltpu.CompilerParams(dimension_semantics=sem, vmem_limit_bytes=VMEM_LIMIT)


_NN = (((1,), (0,)), ((), ()))
_NT = (((1,), (1,)), ((), ()))
_TN = (((0,), (0,)), ((), ()))


def _bdot(a, b):
    return jnp.dot(a.astype(BF16), b.astype(BF16), preferred_element_type=F32)


def _bdot_nt(a, b):
    return lax.dot_general(a.astype(BF16), b.astype(BF16), _NT, preferred_element_type=F32)


def _split2(x):
    hi = x.astype(BF16)
    lo = (x - hi.astype(F32)).astype(BF16)
    return hi, lo


def _split3(x):
    hi = x.astype(BF16)
    r1 = x - hi.astype(F32)
    mid = r1.astype(BF16)
    lo = (r1 - mid.astype(F32)).astype(BF16)
    return hi, mid, lo


def _rmsnorm_kernel(x_ref, g_ref, o_ref):
    x = x_ref[...]
    ms = jnp.mean(x * x, axis=-1, keepdims=True)
    o_ref[...] = (x * lax.rsqrt(ms + RMS_EPS) * g_ref[...]).astype(o_ref.dtype)


def rmsnorm_rows(x, g):
    m, d = x.shape
    return pl.pallas_call(
        _rmsnorm_kernel,
        grid=(m // ROW_TILE,),
        in_specs=[pl.BlockSpec((ROW_TILE, d), lambda i: (i, 0)),
                  pl.BlockSpec((1, d), lambda i: (0, 0))],
        out_specs=pl.BlockSpec((ROW_TILE, d), lambda i: (i, 0)),
        out_shape=jax.ShapeDtypeStruct((m, d), BF16),
        compiler_params=_params("parallel"),
        name="rmsnorm_rows",
    )(x, g.reshape(1, d))


def _mm_kernel(*refs, n_in, has_res):
    o_ref = refs[-1]
    acc = None
    for a_ref, w_ref in zip(refs[:n_in], refs[n_in:2 * n_in]):
        t = jnp.dot(a_ref[...].astype(BF16), w_ref[...], preferred_element_type=F32)
        acc = t if acc is None else acc + t
    if has_res:
        acc = acc + refs[2 * n_in][...]
    o_ref[...] = acc


def matmul_rows(a_list, w_list, res=None, *, tn):
    m = a_list[0].shape[0]
    n = w_list[0].shape[1]
    n_in = len(a_list)
    in_specs = [pl.BlockSpec((ROW_TILE, a.shape[1]), lambda j, i: (i, 0)) for a in a_list]
    in_specs += [pl.BlockSpec((w.shape[0], tn), lambda j, i: (0, j)) for w in w_list]
    args = list(a_list) + list(w_list)
    if res is not None:
        in_specs.append(pl.BlockSpec((ROW_TILE, tn), lambda j, i: (i, j)))
        args.append(res)
    return pl.pallas_call(
        functools.partial(_mm_kernel, n_in=n_in, has_res=res is not None),
        grid=(n // tn, m // ROW_TILE),
        in_specs=in_specs,
        out_specs=pl.BlockSpec((ROW_TILE, tn), lambda j, i: (i, j)),
        out_shape=jax.ShapeDtypeStruct((m, n), F32),
        compiler_params=_params("parallel", "parallel"),
        name="matmul_rows",
    )(*args)


def _qknorm_kernel(p_ref, qg_ref, kg_ref, q_ref, k_ref, *, n_heads):
    for h in range(n_heads):
        for src, g_ref, dst in ((h, qg_ref, q_ref), (n_heads + h, kg_ref, k_ref)):
            x = p_ref[:, src * DH_A:(src + 1) * DH_A]
            ms = jnp.mean(x * x, axis=-1, keepdims=True)
            dst[:, h * DH_A:(h + 1) * DH_A] = x * lax.rsqrt(ms + RMS_EPS) * g_ref[...]


def qk_norm(p, q_g, k_g, d_a):
    m = p.shape[0]
    n_heads = d_a // DH_A
    return pl.pallas_call(
        functools.partial(_qknorm_kernel, n_heads=n_heads),
        grid=(m // ROW_TILE,),
        in_specs=[pl.BlockSpec((ROW_TILE, 2 * d_a), lambda i: (i, 0)),
                  pl.BlockSpec((1, DH_A), lambda i: (0, 0)),
                  pl.BlockSpec((1, DH_A), lambda i: (0, 0))],
        out_specs=[pl.BlockSpec((ROW_TILE, d_a), lambda i: (i, 0)),
                   pl.BlockSpec((ROW_TILE, d_a), lambda i: (i, 0))],
        out_shape=[jax.ShapeDtypeStruct((m, d_a), F32), jax.ShapeDtypeStruct((m, d_a), F32)],
        compiler_params=_params("parallel"),
        name="qk_norm",
    )(p, q_g.reshape(1, DH_A), k_g.reshape(1, DH_A))


def _strict_after(n):
    r = lax.broadcasted_iota(I32, (n, n), 0)
    c = lax.broadcasted_iota(I32, (n, n), 1)
    return (r > c).astype(BF16)


def _sb_logits(z2, after, mask):
    ls = jnp.minimum(z2, 0.0) - jnp.log(1.0 + jnp.exp2(-jnp.abs(z2))) * LOG2_E
    lk = ls - z2
    if mask is not None:
        lk = jnp.where(mask, lk, 0.0)
        ls = jnp.where(mask, ls, -jnp.inf)
    hi, lo = _split2(lk)
    tail = (jnp.dot(hi, after, preferred_element_type=F32)
            + jnp.dot(lo, after, preferred_element_type=F32))
    return ls, tail, tail[:, :1] + lk[:, :1]


def _sb_weights(ls, tail, run):
    return jnp.exp2(ls + tail + run).astype(BF16)


def _attn_kernel(bias_ref, q_ref, k_ref, v_ref, beta_ref, o_ref, *, blk):
    h = pl.program_id(0)
    i = pl.program_id(1)
    q = (q_ref[...] * (DH_A ** -0.5 * LOG2_E)).astype(BF16)
    bias = bias_ref[h] * LOG2_E
    after = _strict_after(blk)

    def logits(j, mask):
        kb = k_ref[pl.ds(pl.multiple_of(j * blk, blk), blk), :].astype(BF16)
        return _sb_logits(lax.dot_general(q, kb, _NT, preferred_element_type=F32) + bias, after, mask)

    def values(j):
        return v_ref[pl.ds(pl.multiple_of(j * blk, blk), blk), :].astype(BF16)

    def two_blocks(j_hi, j_lo, acc, run, mask_hi, mask_lo):
        ls_hi, tail_hi, sum_hi = logits(j_hi, mask_hi)
        ls_lo, tail_lo, sum_lo = logits(j_lo, mask_lo)
        w_hi = _sb_weights(ls_hi, tail_hi, run)
        w_lo = _sb_weights(ls_lo, tail_lo, run + sum_hi)
        acc = (acc + jnp.dot(w_hi, values(j_hi), preferred_element_type=F32)
               + jnp.dot(w_lo, values(j_lo), preferred_element_type=F32))
        return acc, run + sum_hi + sum_lo

    q_pos = lax.broadcasted_iota(I32, (blk, 1), 0)
    k_pos = lax.broadcasted_iota(I32, (1, blk), 1)
    odd = jnp.broadcast_to((i % 2) == 1, (blk, blk))
    acc, run = two_blocks(i, jnp.maximum(i - 1, 0), jnp.zeros((blk, DH_A), F32), jnp.zeros((blk, 1), F32),
                          k_pos < q_pos, odd)

    def body(t, carry):
        m = i // 2 - 1 - t
        return two_blocks(2 * m + 1, 2 * m, carry[0], carry[1], None, None)

    acc, _ = lax.fori_loop(0, i // 2, body, (acc, run))
    o_ref[...] = acc * beta_ref[...]


def sb_attention_prompt(q, k, p, bias, beta):
    t, d_a = q.shape
    n_heads = d_a // DH_A
    blk = ROW_TILE
    return pl.pallas_call(
        functools.partial(_attn_kernel, blk=blk),
        grid=(n_heads, t // blk),
        in_specs=[pl.BlockSpec(memory_space=pltpu.SMEM),
                  pl.BlockSpec((blk, DH_A), lambda h, i: (i, h)),
                  pl.BlockSpec((t, DH_A), lambda h, i: (0, h)),
                  pl.BlockSpec((t, DH_A), lambda h, i: (0, 2 * n_heads + h)),
                  pl.BlockSpec((1, DH_A), lambda h, i: (0, h))],
        out_specs=pl.BlockSpec((blk, DH_A), lambda h, i: (i, h)),
        out_shape=jax.ShapeDtypeStruct((t, d_a), F32),
        compiler_params=_params("parallel", "parallel"),
        name="sb_attention_prompt",
    )(bias, q, k, p, beta.reshape(1, d_a))


def _attn_sample_kernel(pt_ref, q_ref, bias_ref, beta_ref, *refs, n_heads, n_pg):
    k_refs, v_refs = refs[:n_pg], refs[n_pg:2 * n_pg]
    o_ref, acc_ref, run_ref = refs[2 * n_pg:]
    j = pl.program_id(1)
    d_a = n_heads * DH_A
    head_of_lane = lax.broadcasted_iota(I32, (n_heads, d_a), 1) // DH_A
    own = head_of_lane == lax.broadcasted_iota(I32, (n_heads, d_a), 0)

    @pl.when(j == 0)
    def _():
        acc_ref[...] = jnp.zeros_like(acc_ref)
        run_ref[...] = jnp.zeros_like(run_ref)

    def page(ref):
        return jnp.concatenate([ref[pl.ds(hh, PAGE_SIZE, stride=n_heads), :] for hh in range(n_heads)],
                               axis=1).astype(BF16)

    q_rows = jnp.where(own, jnp.broadcast_to(q_ref[0] * (DH_A ** -0.5 * LOG2_E), (n_heads, d_a)), 0.0).astype(BF16)
    after = _strict_after(PAGE_SIZE)
    bias = bias_ref[...] * LOG2_E
    zs = [lax.dot_general(q_rows, page(k_ref), _NT, preferred_element_type=F32) + bias
          for k_ref in k_refs]
    parts = [_sb_logits(z, after, None) for z in zs]
    run = run_ref[...]
    ws = []
    for ls, tail, lk_sum in parts:
        ws.append(_sb_weights(ls, tail, run))
        run = run + lk_sum
    acc = acc_ref[...]
    for w, v_ref in zip(ws, v_refs):
        acc = acc + jnp.dot(w, page(v_ref), preferred_element_type=F32)
    acc_ref[...] = acc
    run_ref[...] = run

    @pl.when(j == pl.num_programs(1) - 1)
    def _():
        o = jnp.sum(jnp.where(own, acc, 0.0), axis=0, keepdims=True)
        o_ref[0] = o * beta_ref[...]


def sb_attention_sample(q, cache_k, cache_v, page_table, bias, beta):
    b, d_a = q.shape
    n_heads = d_a // DH_A
    n_pages = page_table.shape[1]
    n_pg = next(p for p in (8, 4, 2, 1) if n_pages % p == 0)

    def page_map(p):
        return lambda bi, j, pt: (pt[bi * n_pages + (n_pages - 1 - (j * n_pg + p))], 0)

    pages = [pl.BlockSpec((PAGE_SIZE * n_heads, DH_A), page_map(p)) for p in range(n_pg)]
    grid_spec = pltpu.PrefetchScalarGridSpec(
        num_scalar_prefetch=1,
        grid=(b, n_pages // n_pg),
        in_specs=[pl.BlockSpec((1, 1, d_a), lambda bi, j, pt: (bi, 0, 0)),
                  pl.BlockSpec((n_heads, 1), lambda bi, j, pt: (0, 0)),
                  pl.BlockSpec((1, d_a), lambda bi, j, pt: (0, 0))] + pages + pages,
        out_specs=pl.BlockSpec((1, 1, d_a), lambda bi, j, pt: (bi, 0, 0)),
        scratch_shapes=[pltpu.VMEM((n_heads, d_a), F32), pltpu.VMEM((n_heads, 1), F32)],
    )
    out = pl.pallas_call(
        functools.partial(_attn_sample_kernel, n_heads=n_heads, n_pg=n_pg),
        grid_spec=grid_spec,
        out_shape=jax.ShapeDtypeStruct((b, 1, d_a), F32),
        compiler_params=_params("parallel", "arbitrary"),
        name="sb_attention_sample",
    )(page_table.reshape(-1), q.reshape(b, 1, d_a), bias.reshape(n_heads, 1), beta.reshape(1, d_a),
      *([cache_k] * n_pg), *([cache_v] * n_pg))
    return out.reshape(b, d_a)


def _seg_sum(x, seg):
    hi, lo = _split2(x)
    return jnp.dot(hi, seg, preferred_element_type=F32) + jnp.dot(lo, seg, preferred_element_type=F32)


def _seg_bcast(y, seg_t):
    hi, mid, lo = _split3(y)
    return (jnp.dot(hi, seg_t, preferred_element_type=F32)
            + jnp.dot(mid, seg_t, preferred_element_type=F32)
            + jnp.dot(lo, seg_t, preferred_element_type=F32))


def _head_segments(d_b):
    assert d_b // HS_B <= LANES
    seg = (jnp.arange(d_b)[:, None] // HS_B == jnp.arange(LANES)[None, :]).astype(BF16)
    return seg, seg.T


def _rwkv_prep_kernel(p_ref, prev_ref, mu_ref, w0_ref, a0_ref, kk_ref, ka_ref,
                      wup_ref, aup_ref, gup_ref, seg_ref, segt_ref,
                      r_ref, k_ref, v_ref, kn_ref, b_ref, ld_ref, g_ref, *, d_b):
    p = p_ref[...]
    xs = p + mu_ref[...] * (prev_ref[...] - p)
    r_ref[...] = xs[:, :d_b]
    k = xs[:, d_b:2 * d_b]
    v_ref[...] = xs[:, 2 * d_b:3 * d_b]
    o = 3 * d_b
    w_lo = xs[:, o:o + W_LORA]
    a_lo = xs[:, o + W_LORA:o + W_LORA + A_LORA]
    g_lo = xs[:, o + W_LORA + A_LORA:o + W_LORA + A_LORA + G_LORA]
    w = -jax.nn.softplus(-(w0_ref[...] + _bdot(jnp.tanh(w_lo), wup_ref[...]))) - 0.5
    ld_ref[...] = -jnp.exp(w)
    a = jax.nn.sigmoid(a0_ref[...] + _bdot(a_lo, aup_ref[...]))
    g_ref[...] = _bdot(jax.nn.sigmoid(g_lo), gup_ref[...])
    kk = k * kk_ref[...]
    inv = lax.rsqrt(_seg_sum(kk * kk, seg_ref[...]) + L2_EPS)
    kk = kk * _seg_bcast(inv, segt_ref[...])
    k_ref[...] = k * (1.0 + (a - 1.0) * ka_ref[...])
    kn_ref[...] = kk
    b_ref[...] = kk * a


def rwkv_prep(pb, prev, lw):
    m, d_shift = pb.shape
    d_b = lw["w0"].shape[0]
    seg, seg_t = _head_segments(d_b)
    row = lambda n: pl.BlockSpec((ROW_TILE, n), lambda i: (i, 0))
    full = lambda a, b: pl.BlockSpec((a, b), lambda i: (0, 0))
    return pl.pallas_call(
        functools.partial(_rwkv_prep_kernel, d_b=d_b),
        grid=(m // ROW_TILE,),
        in_specs=[row(d_shift), row(d_shift), full(1, d_shift), full(1, d_b), full(1, d_b),
                  full(1, d_b), full(1, d_b), full(W_LORA, d_b), full(A_LORA, d_b), full(G_LORA, d_b),
                  full(d_b, LANES), full(LANES, d_b)],
        out_specs=[row(d_b)] * 7,
        out_shape=[jax.ShapeDtypeStruct((m, d_b), F32)] * 7,
        compiler_params=_params("parallel"),
        name="rwkv_prep",
    )(pb, prev, lw["mu_shift"].reshape(1, -1), lw["w0"].reshape(1, -1), lw["a0"].reshape(1, -1),
      lw["k_k"].reshape(1, -1), lw["k_a"].reshape(1, -1), lw["w_up"].astype(BF16), lw["a_up"].astype(BF16),
      lw["g_up"].astype(BF16), seg, seg_t)


def _d3(a, b, dims=_NN):
    dg = lambda x, y: lax.dot_general(x, y, dims, preferred_element_type=F32)
    return dg(a[0], b[0]) + dg(a[0], b[1]) + dg(a[1], b[0])


def _chunk_affine(heads, incl_b, incl, strict):
    c, k_dim = heads[0][0].shape
    sp = _split2
    each = lambda f, *lists: [f(*args) for args in zip(*lists)]
    r, k, v, kk, beta, ld = (list(col) for col in zip(*heads))
    cum = each(lambda x: sum(jnp.dot(incl_b, part, preferred_element_type=F32) for part in _split3(x)), ld)
    e_in = each(jnp.exp, cum)
    e_neg = each(lambda x: jnp.exp(-x), cum)
    kk_d = each(lambda a, x, l: a * jnp.exp(x - l), kk, cum, ld)
    r_d = each(lambda a, e: a * e, r, e_in)
    right = each(lambda a, b, e: sp(jnp.concatenate([a * e, b * e], axis=0)), k, beta, e_neg)
    gram = each(lambda a, b, rt: _d3(sp(jnp.concatenate([a, b], axis=0)), rt, _NT), kk_d, r_d, right)
    v_s = each(sp, v)
    x = each(lambda g, a, vs: jnp.concatenate([a, _d3(sp(g[:c, :c] * strict), vs)], axis=1), gram, kk_d, v_s)
    pw = each(lambda g: sp(-(g[:c, c:] * strict)), gram)
    n = 1
    while True:
        x = each(lambda a, b: a + _d3(b, sp(a)), x, pw)
        n *= 2
        if n >= c:
            break
        pw = each(lambda b: sp(_d3(b, b)), pw)
    x_s = each(sp, x)
    mbx = each(lambda g, xs: _d3(sp(g[c:, c:] * incl), xs), gram, x_s)
    r1 = each(lambda a, b: a - b[:, :k_dim], r_d, mbx)
    y0 = each(lambda g, vs, b: _d3(sp(g[c:, :c] * incl), vs) - b[:, k_dim:], gram, v_s, mbx)
    xtb = each(lambda xs, rt: _d3(xs, (rt[0][c:], rt[1][c:]), _TN), x_s, right)
    eye = (lax.broadcasted_iota(I32, (k_dim, k_dim), 0)
           == lax.broadcasted_iota(I32, (k_dim, k_dim), 1)).astype(F32)
    p = each(lambda t, e: (eye - t[:k_dim]) * e[c - 1:c, :], xtb, e_in)
    q = each(lambda vs, rt, t, e: (_d3(vs, (rt[0][:c], rt[1][:c]), _TN) - t[k_dim:]) * e[c - 1:c, :],
             v_s, right, xtb, e_in)
    return r1, y0, p, q


def _rwkv_chunks_kernel(valid_ref, r_ref, k_ref, v_ref, kk_ref, b_ref, ld_ref,
                        r1_ref, y0_ref, p_ref, q_ref, *, n_chunks, n_heads):
    chunk = HS_B
    row = lax.broadcasted_iota(I32, (chunk, chunk), 0)
    col = lax.broadcasted_iota(I32, (chunk, chunk), 1)
    incl = (col <= row).astype(F32)
    strict = (col < row).astype(F32)
    incl_b = incl.astype(BF16)

    def body(ci, carry):
        rows = pl.ds(pl.multiple_of(ci * chunk, chunk), chunk)
        valid = valid_ref[rows, :]
        lanes = [slice(hh * HS_B, (hh + 1) * HS_B) for hh in range(n_heads)]
        heads = [(r_ref[rows, ln], k_ref[rows, ln] * valid, v_ref[rows, ln] * valid, kk_ref[rows, ln] * valid,
                  b_ref[rows, ln] * valid, ld_ref[rows, ln] * valid) for ln in lanes]
        for ln, r1, y0, p, q in zip(lanes, *_chunk_affine(heads, incl_b, incl, strict)):
            r1_ref[rows, ln] = r1
            y0_ref[rows, ln] = y0
            p_ref[rows, ln] = p
            q_ref[rows, ln] = q
        return carry

    lax.fori_loop(0, n_chunks, body, 0)


def rwkv_chunks(valid, r, k, v, kk, beta, ld):
    m, d_b = r.shape
    heads_per_step = 8
    width = heads_per_step * HS_B
    seq = pl.BlockSpec((ROW_TILE, width), lambda g, i: (i, g))
    return pl.pallas_call(
        functools.partial(_rwkv_chunks_kernel, n_chunks=ROW_TILE // HS_B, n_heads=heads_per_step),
        grid=(d_b // width, m // ROW_TILE),
        in_specs=[pl.BlockSpec((ROW_TILE, 1), lambda g, i: (i, 0))] + [seq] * 6,
        out_specs=[seq] * 4,
        out_shape=[jax.ShapeDtypeStruct((m, d_b), F32)] * 4,
        compiler_params=_params("parallel", "parallel"),
        name="rwkv_chunks",
    )(valid, r, k, v, kk, beta, ld)


def _rwkv_state_kernel(r1_ref, y0_ref, p_ref, q_ref, s0_ref, y_ref, s_ref, state_ref, *, n_chunks, n_heads):
    i = pl.program_id(0)
    chunk = HS_B

    @pl.when(i == 0)
    def _():
        state_ref[...] = s0_ref[...]

    def body(ci, carry):
        rows = pl.ds(pl.multiple_of(ci * chunk, chunk), chunk)
        lanes = [slice(hh * HS_B, (hh + 1) * HS_B) for hh in range(n_heads)]
        states = [_split2(state_ref[hh]) for hh in range(n_heads)]
        for hh, ln in enumerate(lanes):
            state_ref[hh] = _d3(states[hh], _split2(p_ref[rows, ln])) + q_ref[rows, ln]
        for hh, ln in enumerate(lanes):
            y_ref[rows, ln] = _d3(_split2(r1_ref[rows, ln]), states[hh], _NT) + y0_ref[rows, ln]
        return carry

    lax.fori_loop(0, n_chunks, body, 0)

    @pl.when(i == pl.num_programs(0) - 1)
    def _():
        s_ref[...] = state_ref[...]


def rwkv_state(r1, y0, p, q, s0):
    m, d_b = r1.shape
    n_heads = d_b // HS_B
    seq = pl.BlockSpec((ROW_TILE, d_b), lambda i: (i, 0))
    st = pl.BlockSpec((n_heads, HS_B, HS_B), lambda i: (0, 0, 0))
    return pl.pallas_call(
        functools.partial(_rwkv_state_kernel, n_chunks=ROW_TILE // HS_B, n_heads=n_heads),
        grid=(m // ROW_TILE,),
        in_specs=[seq] * 4 + [st],
        out_specs=[seq, st],
        out_shape=[jax.ShapeDtypeStruct((m, d_b), F32), jax.ShapeDtypeStruct(s0.shape, F32)],
        scratch_shapes=[pltpu.VMEM((n_heads, HS_B, HS_B), F32)],
        compiler_params=_params("arbitrary"),
        name="rwkv_state",
    )(r1, y0, p, q, s0)


def _rwkv_step_kernel(r_ref, k_ref, v_ref, kk_ref, b_ref, ld_ref, s0_ref, y_ref, s_ref, *, n_heads):
    eye = (lax.broadcasted_iota(I32, (HS_B, HS_B), 0) == lax.broadcasted_iota(I32, (HS_B, HS_B), 1))
    for hh in range(n_heads):
        lanes = slice(hh * HS_B, (hh + 1) * HS_B)
        row = lambda ref: ref[0, :, lanes]
        s0 = s0_ref[0, hh]
        s_kk = jnp.sum(s0 * row(kk_ref), axis=1, keepdims=True)
        v_col = jnp.sum(jnp.where(eye, row(v_ref), 0.0), axis=1, keepdims=True)
        s1 = s0 * jnp.exp(row(ld_ref)) - s_kk * row(b_ref) + v_col * row(k_ref)
        y_col = jnp.sum(s1 * row(r_ref), axis=1, keepdims=True)
        y_ref[0, :, lanes] = jnp.sum(jnp.where(eye, y_col, 0.0), axis=0, keepdims=True)
        s_ref[0, hh] = s1


def rwkv_step(r, k, v, kk, beta, ld, s0):
    b, d_b = r.shape
    n_heads = d_b // HS_B
    row = pl.BlockSpec((1, 1, d_b), lambda i: (i, 0, 0))
    st = pl.BlockSpec((1, n_heads, HS_B, HS_B), lambda i: (i, 0, 0, 0))
    y, s = pl.pallas_call(
        functools.partial(_rwkv_step_kernel, n_heads=n_heads),
        grid=(b,),
        in_specs=[row] * 6 + [st],
        out_specs=[row, st],
        out_shape=[jax.ShapeDtypeStruct((b, 1, d_b), F32), jax.ShapeDtypeStruct(s0.shape, F32)],
        compiler_params=_params("parallel"),
        name="rwkv_step",
    )(*(a.reshape(b, 1, d_b) for a in (r, k, v, kk, beta, ld)), s0)
    return y.reshape(b, d_b), s


def _rwkv_post_kernel(y_ref, r_ref, k_ref, v_ref, g_ref, lng_ref, lnb_ref, rk_ref, seg_ref, segt_ref, o_ref):
    seg = seg_ref[...]
    seg_t = segt_ref[...]
    y = y_ref[...]
    inv_n = 1.0 / HS_B
    d = y - _seg_bcast(_seg_sum(y, seg) * inv_n, seg_t)
    var = _seg_sum(d * d, seg) * inv_n
    yn = d * _seg_bcast(lax.rsqrt(var + GN_EPS), seg_t) * lng_ref[...] + lnb_ref[...]
    bonus = _seg_bcast(_seg_sum(r_ref[...] * k_ref[...] * rk_ref[...], seg), seg_t) * v_ref[...]
    o_ref[...] = (yn + bonus) * g_ref[...]


def rwkv_post(y, r, k, v, g, ln_g, ln_b, r_k):
    m, d_b = y.shape
    seg, seg_t = _head_segments(d_b)
    row = pl.BlockSpec((ROW_TILE, d_b), lambda i: (i, 0))
    par = pl.BlockSpec((1, d_b), lambda i: (0, 0))
    return pl.pallas_call(
        _rwkv_post_kernel,
        grid=(m // ROW_TILE,),
        in_specs=[row] * 5 + [par] * 3 + [pl.BlockSpec((d_b, LANES), lambda i: (0, 0)),
                                          pl.BlockSpec((LANES, d_b), lambda i: (0, 0))],
        out_specs=row,
        out_shape=jax.ShapeDtypeStruct((m, d_b), F32),
        compiler_params=_params("parallel"),
        name="rwkv_post",
    )(y, r, k, v, g, ln_g.reshape(1, d_b), ln_b.reshape(1, d_b), r_k.reshape(1, d_b), seg, seg_t)


def _topk_rows(s, n_out, neg):
    n = s.shape[0]
    rows = lax.broadcasted_iota(I32, s.shape, 0)
    vals, idxs = [], []
    for _ in range(n_out):
        m = jnp.max(s, axis=0, keepdims=True)
        idx = jnp.min(jnp.where(s == m, rows, n), axis=0, keepdims=True)
        s = jnp.where(rows == idx, neg, s)
        vals.append(m)
        idxs.append(idx)
    return vals, idxs


_CAND_LEN = [PEER_TOPK // (a + 1) for a in range(PEER_TOPK)]
_CAND_OFF = [sum(_CAND_LEN[:a]) for a in range(PEER_TOPK)]
_CAND_ROWS = -(-sum(_CAND_LEN) // 8) * 8


def _peer_topk_kernel(q_ref, keys_ref, e1_ref, e2_ref, gate_ref):
    neg = -jnp.inf
    dq = keys_ref.shape[-1]
    n_rows = q_ref.shape[0]
    e1_all, e2_all, gate_all = [], [], []
    for h in range(PEER_HEADS):
        tops = []
        for c in range(2):
            qc = q_ref[:, (2 * h + c) * dq:(2 * h + c + 1) * dq]
            s_t = _bdot_nt(keys_ref[h, c], qc)
            tops.append(_topk_rows(s_t, PEER_TOPK, neg))
        (v0, i0), (v1, i1) = tops
        cand = [v0[a] + v1[b] for a in range(PEER_TOPK) for b in range(_CAND_LEN[a])]
        cand.append(jnp.full((_CAND_ROWS - len(cand), n_rows), neg, F32))
        c_val, c_idx = _topk_rows(jnp.concatenate(cand, axis=0), PEER_TOPK, neg)
        pos = jnp.concatenate(c_idx, axis=0)
        a_sel = jnp.zeros_like(pos)
        off = jnp.zeros_like(pos)
        for a in range(1, PEER_TOPK):
            past = pos >= _CAND_OFF[a]
            a_sel = a_sel + past.astype(I32)
            off = off + jnp.where(past, _CAND_LEN[a - 1], 0)
        b_sel = pos - off
        e1 = jnp.zeros_like(pos)
        e2 = jnp.zeros_like(pos)
        for a in range(PEER_TOPK):
            e1 = jnp.where(a_sel == a, i0[a], e1)
            e2 = jnp.where(b_sel == a, i1[a], e2)
        ex = jnp.exp(jnp.concatenate(c_val, axis=0) - c_val[0])
        e1_all.append(e1)
        e2_all.append(e2)
        gate_all.append(ex / jnp.sum(ex, axis=0, keepdims=True))
    e1_ref[...] = jnp.concatenate(e1_all, axis=0).T
    e2_ref[...] = jnp.concatenate(e2_all, axis=0).T
    gate_ref[...] = jnp.concatenate(gate_all, axis=0).T


def peer_topk(q, subkeys):
    m = q.shape[0]
    n_sel = PEER_HEADS * PEER_TOPK
    rows = LANES
    out = pl.BlockSpec((rows, n_sel), lambda i: (i, 0))
    return pl.pallas_call(
        _peer_topk_kernel,
        grid=(m // rows,),
        in_specs=[pl.BlockSpec((rows, q.shape[1]), lambda i: (i, 0)),
                  pl.BlockSpec(subkeys.shape, lambda i: (0, 0, 0, 0))],
        out_specs=[out, out, out],
        out_shape=[jax.ShapeDtypeStruct((m, n_sel), I32), jax.ShapeDtypeStruct((m, n_sel), I32),
                   jax.ShapeDtypeStruct((m, n_sel), F32)],
        compiler_params=_params("parallel"),
        name="peer_topk",
    )(q, subkeys)


def _peer_act_kernel(h_ref, u_ref, e1_ref, e2_ref, gate_ref, coef_ref, pre_ref):
    s = pl.program_id(1)

    @pl.when(s == 0)
    def _():
        pre_ref[...] = jnp.zeros_like(pre_ref)

    scores = lax.dot_general(h_ref[...], u_ref[...], _NT, preferred_element_type=F32)
    e1 = e1_ref[...]
    e2 = e2_ref[...]
    pre = pre_ref[...]
    for t in range(EXPERT_CHUNK // N_KEYS):
        picked = jnp.take_along_axis(scores[:, t * N_KEYS:(t + 1) * N_KEYS], e2, axis=1)
        pre = jnp.where(e1 == s * (EXPERT_CHUNK // N_KEYS) + t, picked, pre)
    pre_ref[...] = pre

    @pl.when(s == pl.num_programs(1) - 1)
    def _():
        coef_ref[...] = gate_ref[...] * jax.nn.gelu(pre)


def peer_act(hn, u, e1, e2, gate):
    m, d = hn.shape
    n_sel = e1.shape[1]
    sel = pl.BlockSpec((ROW_TILE, n_sel), lambda i, s: (i, 0))
    return pl.pallas_call(
        _peer_act_kernel,
        grid=(m // ROW_TILE, u.shape[0] // EXPERT_CHUNK),
        in_specs=[pl.BlockSpec((ROW_TILE, d), lambda i, s: (i, 0)),
                  pl.BlockSpec((EXPERT_CHUNK, d), lambda i, s: (s, 0)),
                  sel, sel, sel],
        out_specs=sel,
        out_shape=jax.ShapeDtypeStruct((m, n_sel), F32),
        scratch_shapes=[pltpu.VMEM((ROW_TILE, n_sel), F32)],
        compiler_params=_params("parallel", "arbitrary"),
        name="peer_act",
    )(hn, u, e1, e2, gate)


def _peer_mix_kernel(x_ref, v_ref, e1_ref, e2_ref, coef_ref, o_ref, w_ref):
    s = pl.program_id(1)
    per_step = EXPERT_CHUNK // N_KEYS

    @pl.when(s == 0)
    def _():
        o_ref[...] = x_ref[...]
        n_sel = e1_ref.shape[1]
        key_iota = lax.broadcasted_iota(I32, (N_KEYS, n_sel), 0)

        def row_body(r, carry):
            e1 = e1_ref[pl.ds(r, 1), :]
            e2 = e2_ref[pl.ds(r, 1), :]
            cf = coef_ref[pl.ds(r, 1), :]
            a_t = jnp.where(key_iota == e1, cf, 0.0).astype(BF16)
            b_t = (key_iota == e2).astype(BF16)
            w_ref[pl.ds(pl.multiple_of(r * N_KEYS, N_KEYS), N_KEYS), :] = lax.dot_general(
                a_t, b_t, _NT, preferred_element_type=F32)
            return carry

        lax.fori_loop(0, x_ref.shape[0], row_body, 0, unroll=8)

    rows = x_ref.shape[0]
    tiles = [w_ref[pl.ds(s * per_step + t, rows, stride=N_KEYS), :].astype(BF16) for t in range(per_step)]
    o_ref[...] += jnp.dot(jnp.concatenate(tiles, axis=1), v_ref[...], preferred_element_type=F32)


def peer_mix(x, v, e1, e2, coef):
    m, d = x.shape
    n_sel = e1.shape[1]
    sel = pl.BlockSpec((ROW_TILE, n_sel), lambda i, s: (i, 0))
    return pl.pallas_call(
        _peer_mix_kernel,
        grid=(m // ROW_TILE, v.shape[0] // EXPERT_CHUNK),
        in_specs=[pl.BlockSpec((ROW_TILE, d), lambda i, s: (i, 0)),
                  pl.BlockSpec((EXPERT_CHUNK, d), lambda i, s: (s, 0)),
                  sel, sel, sel],
        out_specs=pl.BlockSpec((ROW_TILE, d), lambda i, s: (i, 0)),
        out_shape=jax.ShapeDtypeStruct((m, d), F32),
        scratch_shapes=[pltpu.VMEM((ROW_TILE * N_KEYS, N_KEYS), F32)],
        compiler_params=_params("parallel", "arbitrary"),
        name="peer_mix",
    )(x, v, e1, e2, coef)


def _layer(x_all, n_prompt, n_sample, cache_k, cache_v, page_table, wkv_s0, shift_s0, lw):
    m, d_model = x_all.shape
    d_a = lw["beta_a"].shape[0]
    d_b = lw["w0"].shape[0]
    n_heads_b = d_b // HS_B
    d_shift = lw["mu_shift"].shape[0]
    sample = slice(n_prompt, n_prompt + n_sample)

    hn = rmsnorm_rows(x_all, lw["norm_mix_g"])
    p = matmul_rows([hn], [lw["w_in"].astype(BF16)], tn=1280)
    qn, kn = qk_norm(p, lw["q_norm_g"], lw["k_norm_g"], d_a)
    va = p[:, 2 * d_a:3 * d_a]
    pb = p[:, 3 * d_a:]

    o_a = sb_attention_prompt(qn, kn, p, lw["sb_bias"], lw["beta_a"])
    o_s = sb_attention_sample(qn[sample], cache_k, cache_v, page_table, lw["sb_bias"], lw["beta_a"])
    o_a = lax.dynamic_update_slice(o_a, o_s, (n_prompt, 0))

    prev = jnp.concatenate([jnp.zeros((1, d_shift), F32), pb[:-1]], axis=0)
    prev = lax.dynamic_update_slice(prev, shift_s0, (n_prompt, 0))
    r, k, v, kk, beta, ld, g = rwkv_prep(pb, prev, lw)
    valid = (jnp.arange(m) < n_prompt).astype(F32)[:, None]
    y_b, wkv_p = rwkv_state(*rwkv_chunks(valid, r, k, v, kk, beta, ld), jnp.zeros((n_heads_b, HS_B, HS_B), F32))
    y_s, wkv_s = rwkv_step(*(a[sample] for a in (r, k, v, kk, beta, ld)), wkv_s0)
    y_b = lax.dynamic_update_slice(y_b, y_s, (n_prompt, 0))
    y_b = rwkv_post(y_b, r, k, v, g, lw["ln_x_g"], lw["ln_x_b"], lw["r_k"])

    w_out = lw["w_out"].astype(BF16)
    x1 = matmul_rows([o_a, y_b], [w_out[:d_a], w_out[d_a:]], res=x_all, tn=1024)

    hn2 = rmsnorm_rows(x1, lw["norm_ffn_g"])
    q = matmul_rows([hn2], [lw["peer_wq"].astype(BF16)], tn=1024)
    e1, e2, gate = peer_topk(q, lw["peer_subkeys"])
    coef = peer_act(hn2, lw["peer_u"].astype(BF16), e1, e2, gate)
    x2 = peer_mix(x1, lw["peer_v"].astype(BF16), e1, e2, coef)
    return x2, kn, va, wkv_p[None], wkv_s, pb


_LAYER_WEIGHTS = ("norm_mix_g", "w_in", "q_norm_g", "k_norm_g", "beta_a", "sb_bias", "mu_shift", "w0", "w_up",
                  "a0", "a_up", "g_up", "k_k", "k_a", "r_k", "ln_x_g", "ln_x_b", "w_out", "norm_ffn_g",
                  "peer_wq", "peer_subkeys", "peer_u", "peer_v")


def kernel(x_prompt, x_sample, cache_k, cache_v, page_table, state_wkv, state_shift, meta_tokens, norm_mix_g, w_in, q_norm_g, k_norm_g, beta_a, sb_bias, mu_shift, w0, w_up, a0, a_up, g_up, k_k, k_a, r_k, ln_x_g, ln_x_b, w_out, norm_ffn_g, peer_wq, peer_subkeys, peer_u, peer_v):
    weights = dict(zip(_LAYER_WEIGHTS, (norm_mix_g, w_in, q_norm_g, k_norm_g, beta_a, sb_bias, mu_shift, w0, w_up,
                                        a0, a_up, g_up, k_k, k_a, r_k, ln_x_g, ln_x_b, w_out, norm_ffn_g,
                                        peer_wq, peer_subkeys, peer_u, peer_v)))
    batch, seq, d_model = x_prompt.shape
    assert batch == 1 and x_sample.shape[1] == 1
    n_sample = x_sample.shape[0]
    n_prompt = seq + N_META
    depth = w_in.shape[0]
    m = -(-(n_prompt + n_sample) // ROW_TILE) * ROW_TILE
    x_all = jnp.concatenate([meta_tokens.astype(F32), x_prompt[0], x_sample[:, 0],
                             jnp.zeros((m - n_prompt - n_sample, d_model), F32)], axis=0)
    n_phys = cache_k.shape[1]
    d_a = beta_a.shape[1]
    n_heads_a = d_a // DH_A
    outs = [[] for _ in range(8)]
    for l in range(depth):
        lw = {name: w[l] for name, w in weights.items()}
        x_all, kn, va, wkv_p, wkv_s, pb = _layer(
            x_all, n_prompt, n_sample, cache_k[l].reshape(n_phys * PAGE_SIZE * n_heads_a, DH_A),
            cache_v[l].reshape(n_phys * PAGE_SIZE * n_heads_a, DH_A), page_table, state_wkv[l], state_shift[l], lw)
        heads = lambda a, rows: a[rows].reshape(-1, n_heads_a, DH_A)
        prompt = slice(0, n_prompt)
        sample = slice(n_prompt, n_prompt + n_sample)
        outs[0].append(heads(kn, prompt)[None])
        outs[1].append(heads(va, prompt)[None])
        outs[2].append(heads(kn, sample)[:, None])
        outs[3].append(heads(va, sample)[:, None])
        outs[4].append(wkv_p)
        outs[5].append(wkv_s)
        outs[6].append(pb[n_prompt - 1:n_prompt])
        outs[7].append(pb[sample])
    y_prompt = x_all[N_META:n_prompt][None]
    y_sample = x_all[n_prompt:n_prompt + n_sample][:, None]
    return (y_prompt, y_sample) + tuple(jnp.stack(o) for o in outs)
```

```python
import functools

import jax
import jax.numpy as jnp
from jax import lax
from jax.experimental import pallas as pl
from jax.experimental.pallas import tpu as pltpu

F32 = jnp.float32
BF16 = jnp.bfloat16
I32 = jnp.int32

LANES = 128
ROW_TILE = 256
VMEM_LIMIT = 56 * 1024 * 1024

N_META = 16
DH_A = 128
HS_B = 64
W_LORA = 64
A_LORA = 64
G_LORA = 128
N_KEYS = 128
PEER_HEADS = 8
PEER_TOPK = 16
PAGE_SIZE = 128
RMS_EPS = 1e-6
GN_EPS = 64e-5
L2_EPS = 1e-12
EXPERT_CHUNK = 2048
LOG2_E = 1.4426950408889634


def _row_tile(m, most=3):
    return next(k * ROW_TILE for k in range(most, 0, -1) if m % (k * ROW_TILE) == 0)


def _params(*sem):
    return pltpu.CompilerParams(dimension_semantics=sem, vmem_limit_bytes=VMEM_LIMIT)


_NN = (((1,), (0,)), ((), ()))
_NT = (((1,), (1,)), ((), ()))
_TN = (((0,), (0,)), ((), ()))


def _bdot(a, b):
    return jnp.dot(a.astype(BF16), b.astype(BF16), preferred_element_type=F32)


def _bdot_nt(a, b):
    return lax.dot_general(a.astype(BF16), b.astype(BF16), _NT, preferred_element_type=F32)


def _split2(x):
    hi = x.astype(BF16)
    lo = (x - hi.astype(F32)).astype(BF16)
    return hi, lo


def _split3(x):
    hi = x.astype(BF16)
    r1 = x - hi.astype(F32)
    mid = r1.astype(BF16)
    lo = (r1 - mid.astype(F32)).astype(BF16)
    return hi, mid, lo


def _rmsnorm_kernel(x_ref, g_ref, o_ref):
    x = x_ref[...]
    ms = jnp.mean(x * x, axis=-1, keepdims=True)
    o_ref[...] = (x * lax.rsqrt(ms + RMS_EPS) * g_ref[...]).astype(o_ref.dtype)


def rmsnorm_rows(x, g):
    m, d = x.shape
    return pl.pallas_call(
        _rmsnorm_kernel,
        grid=(m // ROW_TILE,),
        in_specs=[pl.BlockSpec((ROW_TILE, d), lambda i: (i, 0)),
                  pl.BlockSpec((1, d), lambda i: (0, 0))],
        out_specs=pl.BlockSpec((ROW_TILE, d), lambda i: (i, 0)),
        out_shape=jax.ShapeDtypeStruct((m, d), BF16),
        compiler_params=_params("parallel"),
        name="rmsnorm_rows",
    )(x, g.reshape(1, d))


def _mm_kernel(*refs, n_in, has_res):
    o_ref = refs[-1]
    acc = None
    for a_ref, w_ref in zip(refs[:n_in], refs[n_in:2 * n_in]):
        t = jnp.dot(a_ref[...].astype(BF16), w_ref[...], preferred_element_type=F32)
        acc = t if acc is None else acc + t
    if has_res:
        acc = acc + refs[2 * n_in][...]
    o_ref[...] = acc


def matmul_rows(a_list, w_list, res=None, *, tn):
    m = a_list[0].shape[0]
    n = w_list[0].shape[1]
    n_in = len(a_list)
    tm = _row_tile(m)
    in_specs = [pl.BlockSpec((tm, a.shape[1]), lambda j, i: (i, 0)) for a in a_list]
    in_specs += [pl.BlockSpec((w.shape[0], tn), lambda j, i: (0, j)) for w in w_list]
    args = list(a_list) + list(w_list)
    if res is not None:
        in_specs.append(pl.BlockSpec((tm, tn), lambda j, i: (i, j)))
        args.append(res)
    return pl.pallas_call(
        functools.partial(_mm_kernel, n_in=n_in, has_res=res is not None),
        grid=(n // tn, m // tm),
        in_specs=in_specs,
        out_specs=pl.BlockSpec((tm, tn), lambda j, i: (i, j)),
        out_shape=jax.ShapeDtypeStruct((m, n), F32),
        compiler_params=_params("parallel", "parallel"),
        name="matmul_rows",
    )(*args)


def _qknorm_kernel(p_ref, qg_ref, kg_ref, q_ref, k_ref, *, n_heads):
    for h in range(n_heads):
        for src, g_ref, dst in ((h, qg_ref, q_ref), (n_heads + h, kg_ref, k_ref)):
            x = p_ref[:, src * DH_A:(src + 1) * DH_A]
            ms = jnp.mean(x * x, axis=-1, keepdims=True)
            dst[:, h * DH_A:(h + 1) * DH_A] = x * lax.rsqrt(ms + RMS_EPS) * g_ref[...]


def qk_norm(p, q_g, k_g, d_a):
    m = p.shape[0]
    n_heads = d_a // DH_A
    return pl.pallas_call(
        functools.partial(_qknorm_kernel, n_heads=n_heads),
        grid=(m // ROW_TILE,),
        in_specs=[pl.BlockSpec((ROW_TILE, 2 * d_a), lambda i: (i, 0)),
                  pl.BlockSpec((1, DH_A), lambda i: (0, 0)),
                  pl.BlockSpec((1, DH_A), lambda i: (0, 0))],
        out_specs=[pl.BlockSpec((ROW_TILE, d_a), lambda i: (i, 0)),
                   pl.BlockSpec((ROW_TILE, d_a), lambda i: (i, 0))],
        out_shape=[jax.ShapeDtypeStruct((m, d_a), F32), jax.ShapeDtypeStruct((m, d_a), F32)],
        compiler_params=_params("parallel"),
        name="qk_norm",
    )(p, q_g.reshape(1, DH_A), k_g.reshape(1, DH_A))


def _strict_after(n):
    r = lax.broadcasted_iota(I32, (n, n), 0)
    c = lax.broadcasted_iota(I32, (n, n), 1)
    return (r > c).astype(BF16)


def _sb_logits(z2, after, mask):
    ls = jnp.minimum(z2, 0.0) - jnp.log(1.0 + jnp.exp2(-jnp.abs(z2))) * LOG2_E
    lk = ls - z2
    if mask is not None:
        lk = jnp.where(mask, lk, 0.0)
        ls = jnp.where(mask, ls, -jnp.inf)
    hi, lo = _split2(lk)
    tail = (jnp.dot(hi, after, preferred_element_type=F32)
            + jnp.dot(lo, after, preferred_element_type=F32))
    return ls, tail, tail[:, :1] + lk[:, :1]


def _sb_weights(ls, tail, run):
    return jnp.exp2(ls + tail + run).astype(BF16)


def _attn_kernel(bias_ref, q_ref, k_ref, v_ref, beta_ref, o_ref, *, blk):
    h = pl.program_id(0)
    i = pl.program_id(1)
    rows = q_ref.shape[0]
    q = (q_ref[...] * (DH_A ** -0.5 * LOG2_E)).astype(BF16)
    bias = bias_ref[h] * LOG2_E
    after = _strict_after(blk)

    def logits(j, mask):
        kb = k_ref[pl.ds(pl.multiple_of(j * blk, blk), blk), :].astype(BF16)
        return _sb_logits(lax.dot_general(q, kb, _NT, preferred_element_type=F32) + bias, after, mask)

    def values(j):
        return v_ref[pl.ds(pl.multiple_of(j * blk, blk), blk), :].astype(BF16)

    def key_blocks(js, acc, run, masks):
        parts = [logits(j, mask) for j, mask in zip(js, masks)]
        ws = []
        for ls, tail, lk_sum in parts:
            ws.append(_sb_weights(ls, tail, run))
            run = run + lk_sum
        for w, j in zip(ws, js):
            acc = acc + jnp.dot(w, values(j), preferred_element_type=F32)
        return acc, run

    nb = rows // blk
    q_pos = i * rows + lax.broadcasted_iota(I32, (rows, 1), 0)
    k_pos = lax.broadcasted_iota(I32, (1, blk), 1)
    top = [nb * i + b for b in reversed(range(nb))]
    acc, run = key_blocks(top, jnp.zeros((rows, DH_A), F32), jnp.zeros((rows, 1), F32),
                          [j * blk + k_pos < q_pos for j in top])

    def body(t, carry):
        m = i - 1 - t
        return key_blocks([nb * m + b for b in reversed(range(nb))], carry[0], carry[1], [None] * nb)

    acc, _ = lax.fori_loop(0, i, body, (acc, run))
    o_ref[...] = acc * beta_ref[...]


def sb_attention_prompt(q, k, p, bias, beta):
    t, d_a = q.shape
    n_heads = d_a // DH_A
    blk = ROW_TILE
    rows = _row_tile(t)
    return pl.pallas_call(
        functools.partial(_attn_kernel, blk=blk),
        grid=(n_heads, t // rows),
        in_specs=[pl.BlockSpec(memory_space=pltpu.SMEM),
                  pl.BlockSpec((rows, DH_A), lambda h, i: (i, h)),
                  pl.BlockSpec((t, DH_A), lambda h, i: (0, h)),
                  pl.BlockSpec((t, DH_A), lambda h, i: (0, 2 * n_heads + h)),
                  pl.BlockSpec((1, DH_A), lambda h, i: (0, h))],
        out_specs=pl.BlockSpec((rows, DH_A), lambda h, i: (i, h)),
        out_shape=jax.ShapeDtypeStruct((t, d_a), F32),
        compiler_params=_params("parallel", "parallel"),
        name="sb_attention_prompt",
    )(bias, q, k, p, beta.reshape(1, d_a))


def _attn_sample_kernel(pt_ref, q_ref, bias_ref, beta_ref, *refs, n_heads, n_pg):
    k_refs, v_refs = refs[:n_pg], refs[n_pg:2 * n_pg]
    o_ref, acc_ref, run_ref = refs[2 * n_pg:]
    j = pl.program_id(1)
    d_a = n_heads * DH_A
    head_of_lane = lax.broadcasted_iota(I32, (n_heads, d_a), 1) // DH_A
    own = head_of_lane == lax.broadcasted_iota(I32, (n_heads, d_a), 0)

    @pl.when(j == 0)
    def _():
        acc_ref[...] = jnp.zeros_like(acc_ref)
        run_ref[...] = jnp.zeros_like(run_ref)

    def page(ref):
        return jnp.concatenate([ref[pl.ds(hh, PAGE_SIZE, stride=n_heads), :] for hh in range(n_heads)],
                               axis=1).astype(BF16)

    q_rows = jnp.where(own, jnp.broadcast_to(q_ref[0] * (DH_A ** -0.5 * LOG2_E), (n_heads, d_a)), 0.0).astype(BF16)
    after = _strict_after(PAGE_SIZE)
    bias = bias_ref[...] * LOG2_E
    zs = [lax.dot_general(q_rows, page(k_ref), _NT, preferred_element_type=F32) + bias
          for k_ref in k_refs]
    parts = [_sb_logits(z, after, None) for z in zs]
    run = run_ref[...]
    ws = []
    for ls, tail, lk_sum in parts:
        ws.append(_sb_weights(ls, tail, run))
        run = run + lk_sum
    acc = acc_ref[...]
    for w, v_ref in zip(ws, v_refs):
        acc = acc + jnp.dot(w, page(v_ref), preferred_element_type=F32)
    acc_ref[...] = acc
    run_ref[...] = run

    @pl.when(j == pl.num_programs(1) - 1)
    def _():
        o = jnp.sum(jnp.where(own, acc, 0.0), axis=0, keepdims=True)
        o_ref[0] = o * beta_ref[...]


def sb_attention_sample(q, cache_k, cache_v, page_table, bias, beta):
    b, d_a = q.shape
    n_heads = d_a // DH_A
    n_pages = page_table.shape[1]
    n_pg = next(p for p in (8, 4, 2, 1) if n_pages % p == 0)

    def page_map(p):
        return lambda bi, j, pt: (pt[bi * n_pages + (n_pages - 1 - (j * n_pg + p))], 0)

    pages = [pl.BlockSpec((PAGE_SIZE * n_heads, DH_A), page_map(p)) for p in range(n_pg)]
    grid_spec = pltpu.PrefetchScalarGridSpec(
        num_scalar_prefetch=1,
        grid=(b, n_pages // n_pg),
        in_specs=[pl.BlockSpec((1, 1, d_a), lambda bi, j, pt: (bi, 0, 0)),
                  pl.BlockSpec((n_heads, 1), lambda bi, j, pt: (0, 0)),
                  pl.BlockSpec((1, d_a), lambda bi, j, pt: (0, 0))] + pages + pages,
        out_specs=pl.BlockSpec((1, 1, d_a), lambda bi, j, pt: (bi, 0, 0)),
        scratch_shapes=[pltpu.VMEM((n_heads, d_a), F32), pltpu.VMEM((n_heads, 1), F32)],
    )
    out = pl.pallas_call(
        functools.partial(_attn_sample_kernel, n_heads=n_heads, n_pg=n_pg),
        grid_spec=grid_spec,
        out_shape=jax.ShapeDtypeStruct((b, 1, d_a), F32),
        compiler_params=_params("parallel", "arbitrary"),
        name="sb_attention_sample",
    )(page_table.reshape(-1), q.reshape(b, 1, d_a), bias.reshape(n_heads, 1), beta.reshape(1, d_a),
      *([cache_k] * n_pg), *([cache_v] * n_pg))
    return out.reshape(b, d_a)


def _seg_sum(x, seg):
    hi, lo = _split2(x)
    return jnp.dot(hi, seg, preferred_element_type=F32) + jnp.dot(lo, seg, preferred_element_type=F32)


def _seg_bcast(y, seg_t):
    hi, mid, lo = _split3(y)
    return (jnp.dot(hi, seg_t, preferred_element_type=F32)
            + jnp.dot(mid, seg_t, preferred_element_type=F32)
            + jnp.dot(lo, seg_t, preferred_element_type=F32))


def _head_segments(d_b):
    assert d_b // HS_B <= LANES
    seg = (jnp.arange(d_b)[:, None] // HS_B == jnp.arange(LANES)[None, :]).astype(BF16)
    return seg, seg.T


def _rwkv_prep_kernel(p_ref, last_ref, shift_ref, mu_ref, w0_ref, a0_ref, kk_ref, ka_ref,
                      wup_ref, aup_ref, gup_ref, seg_ref, segt_ref,
                      r_ref, k_ref, v_ref, kn_ref, b_ref, ld_ref, g_ref, prev_ref,
                      *, d_b, col0, sample_block, sample_row):
    i = pl.program_id(0)
    d_shift = mu_ref.shape[1]
    p = p_ref[:, col0:col0 + d_shift]
    above = jnp.where(i == 0, 0.0, last_ref[last_ref.shape[0] - 1:, col0:col0 + d_shift])
    first = lax.broadcasted_iota(I32, (p.shape[0], 1), 0) == 0
    prev_ref[...] = jnp.where(first, above, pltpu.roll(p, 1, axis=0))

    @pl.when(i == sample_block)
    def _():
        prev_ref[pl.ds(sample_row, shift_ref.shape[0]), :] = shift_ref[...]

    xs = p + mu_ref[...] * (prev_ref[...] - p)
    r_ref[...] = xs[:, :d_b]
    k = xs[:, d_b:2 * d_b]
    v_ref[...] = xs[:, 2 * d_b:3 * d_b]
    o = 3 * d_b
    w_lo = xs[:, o:o + W_LORA]
    a_lo = xs[:, o + W_LORA:o + W_LORA + A_LORA]
    g_lo = xs[:, o + W_LORA + A_LORA:o + W_LORA + A_LORA + G_LORA]
    w = -jax.nn.softplus(-(w0_ref[...] + _bdot(jnp.tanh(w_lo), wup_ref[...]))) - 0.5
    ld_ref[...] = -jnp.exp(w)
    a = jax.nn.sigmoid(a0_ref[...] + _bdot(a_lo, aup_ref[...]))
    g_ref[...] = _bdot(jax.nn.sigmoid(g_lo), gup_ref[...])
    kk = k * kk_ref[...]
    inv = lax.rsqrt(_seg_sum(kk * kk, seg_ref[...]) + L2_EPS)
    kk = kk * _seg_bcast(inv, segt_ref[...])
    k_ref[...] = k * (1.0 + (a - 1.0) * ka_ref[...])
    kn_ref[...] = kk
    b_ref[...] = kk * a


def rwkv_prep(p, col0, shift_s, sample_start, lw):
    m, d_in = p.shape
    d_shift = lw["mu_shift"].shape[0]
    d_b = lw["w0"].shape[0]
    seg, seg_t = _head_segments(d_b)
    sample_block, sample_row = divmod(sample_start, ROW_TILE)
    assert sample_row + shift_s.shape[0] <= ROW_TILE and sample_row % 8 == 0
    sub = 8
    row = lambda n: pl.BlockSpec((ROW_TILE, n), lambda i: (i, 0))
    full = lambda a, b: pl.BlockSpec((a, b), lambda i: (0, 0))
    return pl.pallas_call(
        functools.partial(_rwkv_prep_kernel, d_b=d_b, col0=col0, sample_block=sample_block, sample_row=sample_row),
        grid=(m // ROW_TILE,),
        in_specs=[row(d_in),
                  pl.BlockSpec((sub, d_in), lambda i: (jnp.maximum(i * (ROW_TILE // sub) - 1, 0), 0)),
                  full(*shift_s.shape), full(1, d_shift), full(1, d_b), full(1, d_b),
                  full(1, d_b), full(1, d_b), full(W_LORA, d_b), full(A_LORA, d_b), full(G_LORA, d_b),
                  full(d_b, LANES), full(LANES, d_b)],
        out_specs=[row(d_b)] * 7,
        out_shape=[jax.ShapeDtypeStruct((m, d_b), F32)] * 7,
        scratch_shapes=[pltpu.VMEM((ROW_TILE, d_shift), F32)],
        compiler_params=_params("parallel"),
        name="rwkv_prep",
    )(p, p, shift_s, lw["mu_shift"].reshape(1, -1), lw["w0"].reshape(1, -1), lw["a0"].reshape(1, -1),
      lw["k_k"].reshape(1, -1), lw["k_a"].reshape(1, -1), lw["w_up"].astype(BF16), lw["a_up"].astype(BF16),
      lw["g_up"].astype(BF16), seg, seg_t)


def _d3(a, b, dims=_NN):
    dg = lambda x, y: lax.dot_general(x, y, dims, preferred_element_type=F32)
    return dg(a[0], b[0]) + dg(a[0], b[1]) + dg(a[1], b[0])


def _chunk_affine(heads, incl_b, incl, strict):
    c, k_dim = heads[0][0].shape
    sp = _split2
    each = lambda f, *lists: [f(*args) for args in zip(*lists)]
    r, k, v, kk, beta, ld = (list(col) for col in zip(*heads))
    cum = each(lambda x: sum(jnp.dot(incl_b, part, preferred_element_type=F32) for part in _split3(x)), ld)
    e_in = each(jnp.exp, cum)
    e_neg = each(lambda x: jnp.exp(-x), cum)
    kk_d = each(lambda a, x, l: a * jnp.exp(x - l), kk, cum, ld)
    r_d = each(lambda a, e: a * e, r, e_in)
    right = each(lambda a, b, e: sp(jnp.concatenate([a * e, b * e], axis=0)), k, beta, e_neg)
    gram = each(lambda a, b, rt: _d3(sp(jnp.concatenate([a, b], axis=0)), rt, _NT), kk_d, r_d, right)
    v_s = each(sp, v)
    x = each(lambda g, a, vs: jnp.concatenate([a, _d3(sp(g[:c, :c] * strict), vs)], axis=1), gram, kk_d, v_s)
    pw = each(lambda g: sp(-(g[:c, c:] * strict)), gram)
    n = 1
    while True:
        x = each(lambda a, b: a + _d3(b, sp(a)), x, pw)
        n *= 2
        if n >= c:
            break
        pw = each(lambda b: sp(_d3(b, b)), pw)
    x_s = each(sp, x)
    mbx = each(lambda g, xs: _d3(sp(g[c:, c:] * incl), xs), gram, x_s)
    r1 = each(lambda a, b: a - b[:, :k_dim], r_d, mbx)
    y0 = each(lambda g, vs, b: _d3(sp(g[c:, :c] * incl), vs) - b[:, k_dim:], gram, v_s, mbx)
    xtb = each(lambda xs, rt: _d3(xs, (rt[0][c:], rt[1][c:]), _TN), x_s, right)
    eye = (lax.broadcasted_iota(I32, (k_dim, k_dim), 0)
           == lax.broadcasted_iota(I32, (k_dim, k_dim), 1)).astype(F32)
    p = each(lambda t, e: (eye - t[:k_dim]) * e[c - 1:c, :], xtb, e_in)
    q = each(lambda vs, rt, t, e: (_d3(vs, (rt[0][:c], rt[1][:c]), _TN) - t[k_dim:]) * e[c - 1:c, :],
             v_s, right, xtb, e_in)
    return r1, y0, p, q


def _rwkv_chunks_kernel(valid_ref, r_ref, k_ref, v_ref, kk_ref, b_ref, ld_ref,
                        r1_ref, y0_ref, p_ref, q_ref, *, n_chunks, n_heads):
    chunk = HS_B
    row = lax.broadcasted_iota(I32, (chunk, chunk), 0)
    col = lax.broadcasted_iota(I32, (chunk, chunk), 1)
    incl = (col <= row).astype(F32)
    strict = (col < row).astype(F32)
    incl_b = incl.astype(BF16)

    def body(ci, carry):
        rows = pl.ds(pl.multiple_of(ci * chunk, chunk), chunk)
        valid = valid_ref[rows, :]
        lanes = [slice(hh * HS_B, (hh + 1) * HS_B) for hh in range(n_heads)]
        heads = [(r_ref[rows, ln], k_ref[rows, ln] * valid, v_ref[rows, ln] * valid, kk_ref[rows, ln] * valid,
                  b_ref[rows, ln] * valid, ld_ref[rows, ln] * valid) for ln in lanes]
        for ln, r1, y0, p, q in zip(lanes, *_chunk_affine(heads, incl_b, incl, strict)):
            r1_ref[rows, ln] = r1
            y0_ref[rows, ln] = y0
            p_ref[rows, ln] = p
            q_ref[rows, ln] = q
        return carry

    lax.fori_loop(0, n_chunks, body, 0)


def rwkv_chunks(valid, r, k, v, kk, beta, ld):
    m, d_b = r.shape
    heads_per_step = 16
    width = heads_per_step * HS_B
    seq = pl.BlockSpec((ROW_TILE, width), lambda g, i: (i, g))
    return pl.pallas_call(
        functools.partial(_rwkv_chunks_kernel, n_chunks=ROW_TILE // HS_B, n_heads=heads_per_step),
        grid=(d_b // width, m // ROW_TILE),
        in_specs=[pl.BlockSpec((ROW_TILE, 1), lambda g, i: (i, 0))] + [seq] * 6,
        out_specs=[seq] * 4,
        out_shape=[jax.ShapeDtypeStruct((m, d_b), F32)] * 4,
        compiler_params=_params("parallel", "parallel"),
        name="rwkv_chunks",
    )(valid, r, k, v, kk, beta, ld)


def _rwkv_state_kernel(r1_ref, y0_ref, p_ref, q_ref, s0_ref, y_ref, s_ref, state_ref, *, n_chunks, n_heads):
    i = pl.program_id(0)
    chunk = HS_B

    @pl.when(i == 0)
    def _():
        state_ref[...] = s0_ref[...]

    def body(ci, carry):
        rows = pl.ds(pl.multiple_of(ci * chunk, chunk), chunk)
        lanes = [slice(hh * HS_B, (hh + 1) * HS_B) for hh in range(n_heads)]
        states = [_split2(state_ref[hh]) for hh in range(n_heads)]
        for hh, ln in enumerate(lanes):
            state_ref[hh] = _d3(states[hh], _split2(p_ref[rows, ln])) + q_ref[rows, ln]
        for hh, ln in enumerate(lanes):
            y_ref[rows, ln] = _d3(_split2(r1_ref[rows, ln]), states[hh], _NT) + y0_ref[rows, ln]
        return carry

    lax.fori_loop(0, n_chunks, body, 0)

    @pl.when(i == pl.num_programs(0) - 1)
    def _():
        s_ref[...] = state_ref[...]


def rwkv_state(r1, y0, p, q, s0):
    m, d_b = r1.shape
    n_heads = d_b // HS_B
    seq = pl.BlockSpec((ROW_TILE, d_b), lambda i: (i, 0))
    st = pl.BlockSpec((n_heads, HS_B, HS_B), lambda i: (0, 0, 0))
    return pl.pallas_call(
        functools.partial(_rwkv_state_kernel, n_chunks=ROW_TILE // HS_B, n_heads=n_heads),
        grid=(m // ROW_TILE,),
        in_specs=[seq] * 4 + [st],
        out_specs=[seq, st],
        out_shape=[jax.ShapeDtypeStruct((m, d_b), F32), jax.ShapeDtypeStruct(s0.shape, F32)],
        scratch_shapes=[pltpu.VMEM((n_heads, HS_B, HS_B), F32)],
        compiler_params=_params("arbitrary"),
        name="rwkv_state",
    )(r1, y0, p, q, s0)


def _rwkv_step_kernel(r_ref, k_ref, v_ref, kk_ref, b_ref, ld_ref, s0_ref, y_ref, s_ref, *, n_heads):
    eye = (lax.broadcasted_iota(I32, (HS_B, HS_B), 0) == lax.broadcasted_iota(I32, (HS_B, HS_B), 1))
    for hh in range(n_heads):
        lanes = slice(hh * HS_B, (hh + 1) * HS_B)
        row = lambda ref: ref[0, :, lanes]
        s0 = s0_ref[0, hh]
        s_kk = jnp.sum(s0 * row(kk_ref), axis=1, keepdims=True)
        v_col = jnp.sum(jnp.where(eye, row(v_ref), 0.0), axis=1, keepdims=True)
        s1 = s0 * jnp.exp(row(ld_ref)) - s_kk * row(b_ref) + v_col * row(k_ref)
        y_col = jnp.sum(s1 * row(r_ref), axis=1, keepdims=True)
        y_ref[0, :, lanes] = jnp.sum(jnp.where(eye, y_col, 0.0), axis=0, keepdims=True)
        s_ref[0, hh] = s1


def rwkv_step(r, k, v, kk, beta, ld, s0):
    b, d_b = r.shape
    n_heads = d_b // HS_B
    row = pl.BlockSpec((1, 1, d_b), lambda i: (i, 0, 0))
    st = pl.BlockSpec((1, n_heads, HS_B, HS_B), lambda i: (i, 0, 0, 0))
    y, s = pl.pallas_call(
        functools.partial(_rwkv_step_kernel, n_heads=n_heads),
        grid=(b,),
        in_specs=[row] * 6 + [st],
        out_specs=[row, st],
        out_shape=[jax.ShapeDtypeStruct((b, 1, d_b), F32), jax.ShapeDtypeStruct(s0.shape, F32)],
        compiler_params=_params("parallel"),
        name="rwkv_step",
    )(*(a.reshape(b, 1, d_b) for a in (r, k, v, kk, beta, ld)), s0)
    return y.reshape(b, d_b), s


def _rwkv_post_kernel(y_ref, r_ref, k_ref, v_ref, g_ref, lng_ref, lnb_ref, rk_ref, seg_ref, segt_ref, o_ref):
    seg = seg_ref[...]
    seg_t = segt_ref[...]
    y = y_ref[...]
    inv_n = 1.0 / HS_B
    d = y - _seg_bcast(_seg_sum(y, seg) * inv_n, seg_t)
    var = _seg_sum(d * d, seg) * inv_n
    yn = d * _seg_bcast(lax.rsqrt(var + GN_EPS), seg_t) * lng_ref[...] + lnb_ref[...]
    bonus = _seg_bcast(_seg_sum(r_ref[...] * k_ref[...] * rk_ref[...], seg), seg_t) * v_ref[...]
    o_ref[...] = (yn + bonus) * g_ref[...]


def rwkv_post(y, r, k, v, g, ln_g, ln_b, r_k):
    m, d_b = y.shape
    seg, seg_t = _head_segments(d_b)
    row = pl.BlockSpec((ROW_TILE, d_b), lambda i: (i, 0))
    par = pl.BlockSpec((1, d_b), lambda i: (0, 0))
    return pl.pallas_call(
        _rwkv_post_kernel,
        grid=(m // ROW_TILE,),
        in_specs=[row] * 5 + [par] * 3 + [pl.BlockSpec((d_b, LANES), lambda i: (0, 0)),
                                          pl.BlockSpec((LANES, d_b), lambda i: (0, 0))],
        out_specs=row,
        out_shape=jax.ShapeDtypeStruct((m, d_b), F32),
        compiler_params=_params("parallel"),
        name="rwkv_post",
    )(y, r, k, v, g, ln_g.reshape(1, d_b), ln_b.reshape(1, d_b), r_k.reshape(1, d_b), seg, seg_t)


def _topk_rows(s, n_out, neg):
    n = s.shape[0]
    rows = lax.broadcasted_iota(I32, s.shape, 0)
    vals, idxs = [], []
    for _ in range(n_out):
        m = jnp.max(s, axis=0, keepdims=True)
        idx = jnp.min(jnp.where(s == m, rows, n), axis=0, keepdims=True)
        s = jnp.where(rows == idx, neg, s)
        vals.append(m)
        idxs.append(idx)
    return vals, idxs


_CAND_LEN = [PEER_TOPK // (a + 1) for a in range(PEER_TOPK)]
_CAND_OFF = [sum(_CAND_LEN[:a]) for a in range(PEER_TOPK)]
_CAND_ROWS = -(-sum(_CAND_LEN) // 8) * 8


def _peer_topk_kernel(q_ref, keys_ref, e1_ref, e2_ref, gate_ref):
    neg = -jnp.inf
    dq = keys_ref.shape[-1]
    n_rows = q_ref.shape[0]
    e1_all, e2_all, gate_all = [], [], []
    for h in range(PEER_HEADS):
        tops = []
        for c in range(2):
            qc = q_ref[:, (2 * h + c) * dq:(2 * h + c + 1) * dq]
            s_t = _bdot_nt(keys_ref[h, c], qc)
            tops.append(_topk_rows(s_t, PEER_TOPK, neg))
        (v0, i0), (v1, i1) = tops
        cand = [v0[a] + v1[b] for a in range(PEER_TOPK) for b in range(_CAND_LEN[a])]
        cand.append(jnp.full((_CAND_ROWS - len(cand), n_rows), neg, F32))
        c_val, c_idx = _topk_rows(jnp.concatenate(cand, axis=0), PEER_TOPK, neg)
        pos = jnp.concatenate(c_idx, axis=0)
        a_sel = jnp.zeros_like(pos)
        off = jnp.zeros_like(pos)
        for a in range(1, PEER_TOPK):
            past = pos >= _CAND_OFF[a]
            a_sel = a_sel + past.astype(I32)
            off = off + jnp.where(past, _CAND_LEN[a - 1], 0)
        b_sel = pos - off
        e1 = jnp.zeros_like(pos)
        e2 = jnp.zeros_like(pos)
        for a in range(PEER_TOPK):
            e1 = jnp.where(a_sel == a, i0[a], e1)
            e2 = jnp.where(b_sel == a, i1[a], e2)
        ex = jnp.exp(jnp.concatenate(c_val, axis=0) - c_val[0])
        e1_all.append(e1)
        e2_all.append(e2)
        gate_all.append(ex / jnp.sum(ex, axis=0, keepdims=True))
    e1_ref[...] = jnp.concatenate(e1_all, axis=0).T
    e2_ref[...] = jnp.concatenate(e2_all, axis=0).T
    gate_ref[...] = jnp.concatenate(gate_all, axis=0).T


def peer_topk(q, subkeys):
    m = q.shape[0]
    n_sel = PEER_HEADS * PEER_TOPK
    rows = LANES
    out = pl.BlockSpec((rows, n_sel), lambda i: (i, 0))
    return pl.pallas_call(
        _peer_topk_kernel,
        grid=(m // rows,),
        in_specs=[pl.BlockSpec((rows, q.shape[1]), lambda i: (i, 0)),
                  pl.BlockSpec(subkeys.shape, lambda i: (0, 0, 0, 0))],
        out_specs=[out, out, out],
        out_shape=[jax.ShapeDtypeStruct((m, n_sel), I32), jax.ShapeDtypeStruct((m, n_sel), I32),
                   jax.ShapeDtypeStruct((m, n_sel), F32)],
        compiler_params=_params("parallel"),
        name="peer_topk",
    )(q, subkeys)


def _peer_act_kernel(h_ref, u_ref, e1_ref, e2_ref, gate_ref, coef_ref, pre_ref):
    s = pl.program_id(1)

    @pl.when(s == 0)
    def _():
        pre_ref[...] = jnp.zeros_like(pre_ref)

    scores = lax.dot_general(h_ref[...], u_ref[...], _NT, preferred_element_type=F32)
    e1 = e1_ref[...]
    e2 = e2_ref[...]
    pre = pre_ref[...]
    for t in range(EXPERT_CHUNK // N_KEYS):
        picked = jnp.take_along_axis(scores[:, t * N_KEYS:(t + 1) * N_KEYS], e2, axis=1)
        pre = jnp.where(e1 == s * (EXPERT_CHUNK // N_KEYS) + t, picked, pre)
    pre_ref[...] = pre

    @pl.when(s == pl.num_programs(1) - 1)
    def _():
        coef_ref[...] = gate_ref[...] * jax.nn.gelu(pre)


def peer_act(hn, u, e1, e2, gate):
    m, d = hn.shape
    n_sel = e1.shape[1]
    rows = _row_tile(m)
    sel = pl.BlockSpec((rows, n_sel), lambda i, s: (i, 0))
    return pl.pallas_call(
        _peer_act_kernel,
        grid=(m // rows, u.shape[0] // EXPERT_CHUNK),
        in_specs=[pl.BlockSpec((rows, d), lambda i, s: (i, 0)),
                  pl.BlockSpec((EXPERT_CHUNK, d), lambda i, s: (s, 0)),
                  sel, sel, sel],
        out_specs=sel,
        out_shape=jax.ShapeDtypeStruct((m, n_sel), F32),
        scratch_shapes=[pltpu.VMEM((rows, n_sel), F32)],
        compiler_params=_params("parallel", "arbitrary"),
        name="peer_act",
    )(hn, u, e1, e2, gate)


def _peer_mix_kernel(x_ref, v_ref, e1_ref, e2_ref, coef_ref, o_ref, w_ref):
    s = pl.program_id(1)
    per_step = EXPERT_CHUNK // N_KEYS

    @pl.when(s == 0)
    def _():
        o_ref[...] = x_ref[...]
        n_sel = e1_ref.shape[1]
        key_iota = lax.broadcasted_iota(I32, (N_KEYS, n_sel), 0)

        def row_body(r, carry):
            e1 = e1_ref[pl.ds(r, 1), :]
            e2 = e2_ref[pl.ds(r, 1), :]
            cf = coef_ref[pl.ds(r, 1), :]
            a_t = jnp.where(key_iota == e1, cf, 0.0).astype(BF16)
            b_t = (key_iota == e2).astype(BF16)
            w_ref[pl.ds(pl.multiple_of(r * N_KEYS, N_KEYS), N_KEYS), :] = lax.dot_general(
                a_t, b_t, _NT, preferred_element_type=F32)
            return carry

        lax.fori_loop(0, x_ref.shape[0], row_body, 0, unroll=32)

    rows = x_ref.shape[0]
    tiles = [w_ref[pl.ds(s * per_step + t, rows, stride=N_KEYS), :].astype(BF16) for t in range(per_step)]
    o_ref[...] += jnp.dot(jnp.concatenate(tiles, axis=1), v_ref[...], preferred_element_type=F32)


def peer_mix(x, v, e1, e2, coef):
    m, d = x.shape
    n_sel = e1.shape[1]
    sel = pl.BlockSpec((ROW_TILE, n_sel), lambda i, s: (i, 0))
    return pl.pallas_call(
        _peer_mix_kernel,
        grid=(m // ROW_TILE, v.shape[0] // EXPERT_CHUNK),
        in_specs=[pl.BlockSpec((ROW_TILE, d), lambda i, s: (i, 0)),
                  pl.BlockSpec((EXPERT_CHUNK, d), lambda i, s: (s, 0)),
                  sel, sel, sel],
        out_specs=pl.BlockSpec((ROW_TILE, d), lambda i, s: (i, 0)),
        out_shape=jax.ShapeDtypeStruct((m, d), F32),
        scratch_shapes=[pltpu.VMEM((ROW_TILE * N_KEYS, N_KEYS), F32)],
        compiler_params=_params("parallel", "arbitrary"),
        name="peer_mix",
    )(x, v, e1, e2, coef)


def _layer(x_all, n_prompt, n_sample, cache_k, cache_v, page_table, wkv_s0, shift_s0, lw):
    m, d_model = x_all.shape
    d_a = lw["beta_a"].shape[0]
    d_b = lw["w0"].shape[0]
    n_heads_b = d_b // HS_B
    d_shift = lw["mu_shift"].shape[0]
    sample = slice(n_prompt, n_prompt + n_sample)

    hn = rmsnorm_rows(x_all, lw["norm_mix_g"])
    p = matmul_rows([hn], [lw["w_in"].astype(BF16)], tn=1280)
    qn, kn = qk_norm(p, lw["q_norm_g"], lw["k_norm_g"], d_a)
    va = p[:, 2 * d_a:3 * d_a]
    pb = p[:, 3 * d_a:]

    o_a = sb_attention_prompt(qn, kn, p, lw["sb_bias"], lw["beta_a"])
    o_s = sb_attention_sample(qn[sample], cache_k, cache_v, page_table, lw["sb_bias"], lw["beta_a"])
    o_a = lax.dynamic_update_slice(o_a, o_s, (n_prompt, 0))

    r, k, v, kk, beta, ld, g = rwkv_prep(p, 3 * d_a, shift_s0, n_prompt, lw)
    valid = (jnp.arange(m) < n_prompt).astype(F32)[:, None]
    y_b, wkv_p = rwkv_state(*rwkv_chunks(valid, r, k, v, kk, beta, ld), jnp.zeros((n_heads_b, HS_B, HS_B), F32))
    y_s, wkv_s = rwkv_step(*(a[sample] for a in (r, k, v, kk, beta, ld)), wkv_s0)
    y_b = lax.dynamic_update_slice(y_b, y_s, (n_prompt, 0))
    y_b = rwkv_post(y_b, r, k, v, g, lw["ln_x_g"], lw["ln_x_b"], lw["r_k"])

    w_out = lw["w_out"].astype(BF16)
    x1 = matmul_rows([o_a, y_b], [w_out[:d_a], w_out[d_a:]], res=x_all, tn=1024)

    hn2 = rmsnorm_rows(x1, lw["norm_ffn_g"])
    q = matmul_rows([hn2], [lw["peer_wq"].astype(BF16)], tn=1024)
    e1, e2, gate = peer_topk(q, lw["peer_subkeys"])
    coef = peer_act(hn2, lw["peer_u"].astype(BF16), e1, e2, gate)
    x2 = peer_mix(x1, lw["peer_v"].astype(BF16), e1, e2, coef)
    return x2, kn, va, wkv_p[None], wkv_s, pb


_LAYER_WEIGHTS = ("norm_mix_g", "w_in", "q_norm_g", "k_norm_g", "beta_a", "sb_bias", "mu_shift", "w0", "w_up",
                  "a0", "a_up", "g_up", "k_k", "k_a", "r_k", "ln_x_g", "ln_x_b", "w_out", "norm_ffn_g",
                  "peer_wq", "peer_subkeys", "peer_u", "peer_v")


def kernel(x_prompt, x_sample, cache_k, cache_v, page_table, state_wkv, state_shift, meta_tokens, norm_mix_g, w_in, q_norm_g, k_norm_g, beta_a, sb_bias, mu_shift, w0, w_up, a0, a_up, g_up, k_k, k_a, r_k, ln_x_g, ln_x_b, w_out, norm_ffn_g, peer_wq, peer_subkeys, peer_u, peer_v):
    weights = dict(zip(_LAYER_WEIGHTS, (norm_mix_g, w_in, q_norm_g, k_norm_g, beta_a, sb_bias, mu_shift, w0, w_up,
                                        a0, a_up, g_up, k_k, k_a, r_k, ln_x_g, ln_x_b, w_out, norm_ffn_g,
                                        peer_wq, peer_subkeys, peer_u, peer_v)))
    batch, seq, d_model = x_prompt.shape
    assert batch == 1 and x_sample.shape[1] == 1
    n_sample = x_sample.shape[0]
    n_prompt = seq + N_META
    depth = w_in.shape[0]
    m = -(-(n_prompt + n_sample) // ROW_TILE) * ROW_TILE
    x_all = jnp.concatenate([meta_tokens.astype(F32), x_prompt[0], x_sample[:, 0],
                             jnp.zeros((m - n_prompt - n_sample, d_model), F32)], axis=0)
    n_phys = cache_k.shape[1]
    d_a = beta_a.shape[1]
    n_heads_a = d_a // DH_A
    outs = [[] for _ in range(8)]
    for l in range(depth):
        lw = {name: w[l] for name, w in weights.items()}
        x_all, kn, va, wkv_p, wkv_s, pb = _layer(
            x_all, n_prompt, n_sample, cache_k[l].reshape(n_phys * PAGE_SIZE * n_heads_a, DH_A),
            cache_v[l].reshape(n_phys * PAGE_SIZE * n_heads_a, DH_A), page_table, state_wkv[l], state_shift[l], lw)
        heads = lambda a, rows: a[rows].reshape(-1, n_heads_a, DH_A)
        prompt = slice(0, n_prompt)
        sample = slice(n_prompt, n_prompt + n_sample)
        outs[0].append(heads(kn, prompt)[None])
        outs[1].append(heads(va, prompt)[None])
        outs[2].append(heads(kn, sample)[:, None])
        outs[3].append(heads(va, sample)[:, None])
        outs[4].append(wkv_p)
        outs[5].append(wkv_s)
        outs[6].append(pb[n_prompt - 1:n_prompt])
        outs[7].append(pb[sample])
    y_prompt = x_all[N_META:n_prompt][None]
    y_sample = x_all[n_prompt:n_prompt + n_sample][:, None]
    return (y_prompt, y_sample) + tuple(jnp.stack(o) for o in outs)
```

```python
import functools

import jax
import jax.numpy as jnp
from jax import lax
from jax.experimental import pallas as pl
from jax.experimental.pallas import tpu as pltpu

F32 = jnp.float32
BF16 = jnp.bfloat16
I32 = jnp.int32

LANES = 128
ROW_TILE = 256
VMEM_LIMIT = 56 * 1024 * 1024

N_META = 16
DH_A = 128
HS_B = 64
W_LORA = 64
A_LORA = 64
G_LORA = 128
N_KEYS = 128
PEER_HEADS = 8
PEER_TOPK = 16
PAGE_SIZE = 128
RMS_EPS = 1e-6
GN_EPS = 64e-5
L2_EPS = 1e-12
EXPERT_CHUNK = 2048
LOG2_E = 1.4426950408889634


def _row_tile(m, most=3):
    return next(k * ROW_TILE for k in range(most, 0, -1) if m % (k * ROW_TILE) == 0)


def _params(*sem):
    return pltpu.CompilerParams(dimension_semantics=sem, vmem_limit_bytes=VMEM_LIMIT)


_NN = (((1,), (0,)), ((), ()))
_NT = (((1,), (1,)), ((), ()))
_TN = (((0,), (0,)), ((), ()))


def _bdot(a, b):
    return jnp.dot(a.astype(BF16), b.astype(BF16), preferred_element_type=F32)


def _bdot_nt(a, b):
    return lax.dot_general(a.astype(BF16), b.astype(BF16), _NT, preferred_element_type=F32)


def _split2(x):
    hi = x.astype(BF16)
    lo = (x - hi.astype(F32)).astype(BF16)
    return hi, lo


def _split3(x):
    hi = x.astype(BF16)
    r1 = x - hi.astype(F32)
    mid = r1.astype(BF16)
    lo = (r1 - mid.astype(F32)).astype(BF16)
    return hi, mid, lo


def _rmsnorm_kernel(x_ref, g_ref, o_ref):
    x = x_ref[...]
    ms = jnp.mean(x * x, axis=-1, keepdims=True)
    o_ref[...] = (x * lax.rsqrt(ms + RMS_EPS) * g_ref[...]).astype(o_ref.dtype)


def rmsnorm_rows(x, g):
    m, d = x.shape
    return pl.pallas_call(
        _rmsnorm_kernel,
        grid=(m // ROW_TILE,),
        in_specs=[pl.BlockSpec((ROW_TILE, d), lambda i: (i, 0)),
                  pl.BlockSpec((1, d), lambda i: (0, 0))],
        out_specs=pl.BlockSpec((ROW_TILE, d), lambda i: (i, 0)),
        out_shape=jax.ShapeDtypeStruct((m, d), BF16),
        compiler_params=_params("parallel"),
        name="rmsnorm_rows",
    )(x, g.reshape(1, d))


def _mm_kernel(*refs, n_in, has_res):
    o_ref = refs[-1]
    acc = None
    for a_ref, w_ref in zip(refs[:n_in], refs[n_in:2 * n_in]):
        t = jnp.dot(a_ref[...].astype(BF16), w_ref[...], preferred_element_type=F32)
        acc = t if acc is None else acc + t
    if has_res:
        acc = acc + refs[2 * n_in][...]
    o_ref[...] = acc


def matmul_rows(a_list, w_list, res=None, *, tn):
    m = a_list[0].shape[0]
    n = w_list[0].shape[1]
    n_in = len(a_list)
    tm = _row_tile(m)
    in_specs = [pl.BlockSpec((tm, a.shape[1]), lambda j, i: (i, 0)) for a in a_list]
    in_specs += [pl.BlockSpec((w.shape[0], tn), lambda j, i: (0, j)) for w in w_list]
    args = list(a_list) + list(w_list)
    if res is not None:
        in_specs.append(pl.BlockSpec((tm, tn), lambda j, i: (i, j)))
        args.append(res)
    return pl.pallas_call(
        functools.partial(_mm_kernel, n_in=n_in, has_res=res is not None),
        grid=(n // tn, m // tm),
        in_specs=in_specs,
        out_specs=pl.BlockSpec((tm, tn), lambda j, i: (i, j)),
        out_shape=jax.ShapeDtypeStruct((m, n), F32),
        compiler_params=_params("parallel", "parallel"),
        name="matmul_rows",
    )(*args)


def _qknorm_kernel(p_ref, qg_ref, kg_ref, q_ref, k_ref, *, n_heads):
    for h in range(n_heads):
        for src, g_ref, dst in ((h, qg_ref, q_ref), (n_heads + h, kg_ref, k_ref)):
            x = p_ref[:, src * DH_A:(src + 1) * DH_A]
            ms = jnp.mean(x * x, axis=-1, keepdims=True)
            dst[:, h * DH_A:(h + 1) * DH_A] = x * lax.rsqrt(ms + RMS_EPS) * g_ref[...]


def qk_norm(p, q_g, k_g, d_a):
    m = p.shape[0]
    n_heads = d_a // DH_A
    return pl.pallas_call(
        functools.partial(_qknorm_kernel, n_heads=n_heads),
        grid=(m // ROW_TILE,),
        in_specs=[pl.BlockSpec((ROW_TILE, 2 * d_a), lambda i: (i, 0)),
                  pl.BlockSpec((1, DH_A), lambda i: (0, 0)),
                  pl.BlockSpec((1, DH_A), lambda i: (0, 0))],
        out_specs=[pl.BlockSpec((ROW_TILE, d_a), lambda i: (i, 0)),
                   pl.BlockSpec((ROW_TILE, d_a), lambda i: (i, 0))],
        out_shape=[jax.ShapeDtypeStruct((m, d_a), F32), jax.ShapeDtypeStruct((m, d_a), F32)],
        compiler_params=_params("parallel"),
        name="qk_norm",
    )(p, q_g.reshape(1, DH_A), k_g.reshape(1, DH_A))


def _strict_after(n):
    r = lax.broadcasted_iota(I32, (2 * n, n), 0) % n
    c = lax.broadcasted_iota(I32, (2 * n, n), 1)
    return (r > c).astype(BF16)


def _sb_logits(z2, after, mask):
    ls = jnp.minimum(z2, 0.0) - jnp.log(1.0 + jnp.exp2(-jnp.abs(z2))) * LOG2_E
    lk = ls - z2
    if mask is not None:
        lk = jnp.where(mask, lk, 0.0)
        ls = jnp.where(mask, ls, -jnp.inf)
    tail = jnp.dot(jnp.concatenate(_split2(lk), axis=1), after, preferred_element_type=F32)
    return ls, tail, tail[:, :1] + lk[:, :1]


def _sb_weights(ls, tail, run):
    return jnp.exp2(ls + tail + run).astype(BF16)


def _attn_kernel(bias_ref, q_ref, k_ref, v_ref, beta_ref, o_ref, *, blk):
    h = pl.program_id(0)
    i = pl.program_id(1)
    rows = q_ref.shape[0]
    q = (q_ref[...] * (DH_A ** -0.5 * LOG2_E)).astype(BF16)
    bias = bias_ref[h] * LOG2_E
    after = _strict_after(blk)

    def logits(j, mask):
        kb = k_ref[pl.ds(pl.multiple_of(j * blk, blk), blk), :].astype(BF16)
        return _sb_logits(lax.dot_general(q, kb, _NT, preferred_element_type=F32) + bias, after, mask)

    def values(j):
        return v_ref[pl.ds(pl.multiple_of(j * blk, blk), blk), :].astype(BF16)

    def key_blocks(js, acc, run, masks):
        parts = [logits(j, mask) for j, mask in zip(js, masks)]
        ws = []
        for ls, tail, lk_sum in parts:
            ws.append(_sb_weights(ls, tail, run))
            run = run + lk_sum
        for w, j in zip(ws, js):
            acc = acc + jnp.dot(w, values(j), preferred_element_type=F32)
        return acc, run

    nb = rows // blk
    q_pos = i * rows + lax.broadcasted_iota(I32, (rows, 1), 0)
    k_pos = lax.broadcasted_iota(I32, (1, blk), 1)
    top = [nb * i + b for b in reversed(range(nb))]
    acc, run = key_blocks(top, jnp.zeros((rows, DH_A), F32), jnp.zeros((rows, 1), F32),
                          [j * blk + k_pos < q_pos for j in top])

    def body(t, carry):
        m = i - 1 - t
        return key_blocks([nb * m + b for b in reversed(range(nb))], carry[0], carry[1], [None] * nb)

    acc, _ = lax.fori_loop(0, i, body, (acc, run))
    o_ref[...] = acc * beta_ref[...]


def sb_attention_prompt(q, k, p, bias, beta):
    t, d_a = q.shape
    n_heads = d_a // DH_A
    blk = ROW_TILE
    rows = _row_tile(t)
    return pl.pallas_call(
        functools.partial(_attn_kernel, blk=blk),
        grid=(n_heads, t // rows),
        in_specs=[pl.BlockSpec(memory_space=pltpu.SMEM),
                  pl.BlockSpec((rows, DH_A), lambda h, i: (i, h)),
                  pl.BlockSpec((t, DH_A), lambda h, i: (0, h)),
                  pl.BlockSpec((t, DH_A), lambda h, i: (0, 2 * n_heads + h)),
                  pl.BlockSpec((1, DH_A), lambda h, i: (0, h))],
        out_specs=pl.BlockSpec((rows, DH_A), lambda h, i: (i, h)),
        out_shape=jax.ShapeDtypeStruct((t, d_a), F32),
        compiler_params=_params("parallel", "parallel"),
        name="sb_attention_prompt",
    )(bias, q, k, p, beta.reshape(1, d_a))


def _attn_sample_kernel(pt_ref, q_ref, bias_ref, beta_ref, *refs, n_heads, n_pg):
    k_refs, v_refs = refs[:n_pg], refs[n_pg:2 * n_pg]
    o_ref, acc_ref, run_ref = refs[2 * n_pg:]
    j = pl.program_id(1)
    d_a = n_heads * DH_A
    head_of_lane = lax.broadcasted_iota(I32, (n_heads, d_a), 1) // DH_A
    own = head_of_lane == lax.broadcasted_iota(I32, (n_heads, d_a), 0)

    @pl.when(j == 0)
    def _():
        acc_ref[...] = jnp.zeros_like(acc_ref)
        run_ref[...] = jnp.zeros_like(run_ref)

    def page(ref):
        return jnp.concatenate([ref[pl.ds(hh, PAGE_SIZE, stride=n_heads), :] for hh in range(n_heads)],
                               axis=1).astype(BF16)

    q_rows = jnp.where(own, jnp.broadcast_to(q_ref[0] * (DH_A ** -0.5 * LOG2_E), (n_heads, d_a)), 0.0).astype(BF16)
    after = _strict_after(PAGE_SIZE)
    bias = bias_ref[...] * LOG2_E
    zs = [lax.dot_general(q_rows, page(k_ref), _NT, preferred_element_type=F32) + bias
          for k_ref in k_refs]
    parts = [_sb_logits(z, after, None) for z in zs]
    run = run_ref[...]
    ws = []
    for ls, tail, lk_sum in parts:
        ws.append(_sb_weights(ls, tail, run))
        run = run + lk_sum
    acc = acc_ref[...]
    for w, v_ref in zip(ws, v_refs):
        acc = acc + jnp.dot(w, page(v_ref), preferred_element_type=F32)
    acc_ref[...] = acc
    run_ref[...] = run

    @pl.when(j == pl.num_programs(1) - 1)
    def _():
        o = jnp.sum(jnp.where(own, acc, 0.0), axis=0, keepdims=True)
        o_ref[0] = o * beta_ref[...]


def sb_attention_sample(q, cache_k, cache_v, page_table, bias, beta):
    b, d_a = q.shape
    n_heads = d_a // DH_A
    n_pages = page_table.shape[1]
    n_pg = next(p for p in (8, 4, 2, 1) if n_pages % p == 0)

    def page_map(p):
        return lambda bi, j, pt: (pt[bi * n_pages + (n_pages - 1 - (j * n_pg + p))], 0)

    pages = [pl.BlockSpec((PAGE_SIZE * n_heads, DH_A), page_map(p)) for p in range(n_pg)]
    grid_spec = pltpu.PrefetchScalarGridSpec(
        num_scalar_prefetch=1,
        grid=(b, n_pages // n_pg),
        in_specs=[pl.BlockSpec((1, 1, d_a), lambda bi, j, pt: (bi, 0, 0)),
                  pl.BlockSpec((n_heads, 1), lambda bi, j, pt: (0, 0)),
                  pl.BlockSpec((1, d_a), lambda bi, j, pt: (0, 0))] + pages + pages,
        out_specs=pl.BlockSpec((1, 1, d_a), lambda bi, j, pt: (bi, 0, 0)),
        scratch_shapes=[pltpu.VMEM((n_heads, d_a), F32), pltpu.VMEM((n_heads, 1), F32)],
    )
    out = pl.pallas_call(
        functools.partial(_attn_sample_kernel, n_heads=n_heads, n_pg=n_pg),
        grid_spec=grid_spec,
        out_shape=jax.ShapeDtypeStruct((b, 1, d_a), F32),
        compiler_params=_params("parallel", "arbitrary"),
        name="sb_attention_sample",
    )(page_table.reshape(-1), q.reshape(b, 1, d_a), bias.reshape(n_heads, 1), beta.reshape(1, d_a),
      *([cache_k] * n_pg), *([cache_v] * n_pg))
    return out.reshape(b, d_a)


def _seg_sum(x, seg):
    hi, lo = _split2(x)
    return jnp.dot(hi, seg, preferred_element_type=F32) + jnp.dot(lo, seg, preferred_element_type=F32)


def _seg_bcast(y, seg_t):
    hi, mid, lo = _split3(y)
    return (jnp.dot(hi, seg_t, preferred_element_type=F32)
            + jnp.dot(mid, seg_t, preferred_element_type=F32)
            + jnp.dot(lo, seg_t, preferred_element_type=F32))


def _head_segments(d_b):
    assert d_b // HS_B <= LANES
    seg = (jnp.arange(d_b)[:, None] // HS_B == jnp.arange(LANES)[None, :]).astype(BF16)
    return seg, seg.T


def _rwkv_prep_kernel(p_ref, last_ref, shift_ref, mu_ref, w0_ref, a0_ref, kk_ref, ka_ref,
                      wup_ref, aup_ref, gup_ref, seg_ref, segt_ref,
                      r_ref, k_ref, v_ref, kn_ref, b_ref, ld_ref, g_ref, prev_ref,
                      *, d_b, col0, sample_block, sample_row):
    i = pl.program_id(0)
    d_shift = mu_ref.shape[1]
    p = p_ref[:, col0:col0 + d_shift]
    above = jnp.where(i == 0, 0.0, last_ref[last_ref.shape[0] - 1:, col0:col0 + d_shift])
    first = lax.broadcasted_iota(I32, (p.shape[0], 1), 0) == 0
    prev_ref[...] = jnp.where(first, above, pltpu.roll(p, 1, axis=0))

    @pl.when(i == sample_block)
    def _():
        prev_ref[pl.ds(sample_row, shift_ref.shape[0]), :] = shift_ref[...]

    xs = p + mu_ref[...] * (prev_ref[...] - p)
    r_ref[...] = xs[:, :d_b]
    k = xs[:, d_b:2 * d_b]
    v_ref[...] = xs[:, 2 * d_b:3 * d_b]
    o = 3 * d_b
    w_lo = xs[:, o:o + W_LORA]
    a_lo = xs[:, o + W_LORA:o + W_LORA + A_LORA]
    g_lo = xs[:, o + W_LORA + A_LORA:o + W_LORA + A_LORA + G_LORA]
    w = -jax.nn.softplus(-(w0_ref[...] + _bdot(jnp.tanh(w_lo), wup_ref[...]))) - 0.5
    ld_ref[...] = -jnp.exp(w)
    a = jax.nn.sigmoid(a0_ref[...] + _bdot(a_lo, aup_ref[...]))
    g_ref[...] = _bdot(jax.nn.sigmoid(g_lo), gup_ref[...])
    kk = k * kk_ref[...]
    inv = lax.rsqrt(_seg_sum(kk * kk, seg_ref[...]) + L2_EPS)
    kk = kk * _seg_bcast(inv, segt_ref[...])
    k_ref[...] = k * (1.0 + (a - 1.0) * ka_ref[...])
    kn_ref[...] = kk
    b_ref[...] = kk * a


def rwkv_prep(p, col0, shift_s, sample_start, lw):
    m, d_in = p.shape
    d_shift = lw["mu_shift"].shape[0]
    d_b = lw["w0"].shape[0]
    seg, seg_t = _head_segments(d_b)
    sample_block, sample_row = divmod(sample_start, ROW_TILE)
    assert sample_row + shift_s.shape[0] <= ROW_TILE and sample_row % 8 == 0
    sub = 8
    row = lambda n: pl.BlockSpec((ROW_TILE, n), lambda i: (i, 0))
    full = lambda a, b: pl.BlockSpec((a, b), lambda i: (0, 0))
    return pl.pallas_call(
        functools.partial(_rwkv_prep_kernel, d_b=d_b, col0=col0, sample_block=sample_block, sample_row=sample_row),
        grid=(m // ROW_TILE,),
        in_specs=[row(d_in),
                  pl.BlockSpec((sub, d_in), lambda i: (jnp.maximum(i * (ROW_TILE // sub) - 1, 0), 0)),
                  full(*shift_s.shape), full(1, d_shift), full(1, d_b), full(1, d_b),
                  full(1, d_b), full(1, d_b), full(W_LORA, d_b), full(A_LORA, d_b), full(G_LORA, d_b),
                  full(d_b, LANES), full(LANES, d_b)],
        out_specs=[row(d_b)] * 7,
        out_shape=[jax.ShapeDtypeStruct((m, d_b), F32)] * 7,
        scratch_shapes=[pltpu.VMEM((ROW_TILE, d_shift), F32)],
        compiler_params=_params("parallel"),
        name="rwkv_prep",
    )(p, p, shift_s, lw["mu_shift"].reshape(1, -1), lw["w0"].reshape(1, -1), lw["a0"].reshape(1, -1),
      lw["k_k"].reshape(1, -1), lw["k_a"].reshape(1, -1), lw["w_up"].astype(BF16), lw["a_up"].astype(BF16),
      lw["g_up"].astype(BF16), seg, seg_t)


def _d3(a, b, dims=_NN):
    if dims == _TN:
        return lax.dot_general(jnp.concatenate([a[0], a[0], a[1]], axis=0),
                               jnp.concatenate([b[0], b[1], b[0]], axis=0), dims, preferred_element_type=F32)
    dg = lambda x, y: lax.dot_general(x, y, dims, preferred_element_type=F32)
    m = a[0].shape[0]
    both = dg(jnp.concatenate(a, axis=0), b[0])
    return both[:m] + both[m:] + dg(a[0], b[1])


def _chunk_affine(heads, incl_b, incl, strict):
    c, k_dim = heads[0][0].shape
    sp = _split2
    each = lambda f, *lists: [f(*args) for args in zip(*lists)]
    r, k, v, kk, beta, ld = (list(col) for col in zip(*heads))
    cum = each(lambda x: jnp.dot(incl_b, jnp.concatenate(_split3(x), axis=0), preferred_element_type=F32), ld)
    e_in = each(jnp.exp, cum)
    e_neg = each(lambda x: jnp.exp(-x), cum)
    kk_d = each(lambda a, x, l: a * jnp.exp(x - l), kk, cum, ld)
    r_d = each(lambda a, e: a * e, r, e_in)
    right = each(lambda a, b, e: sp(jnp.concatenate([a * e, b * e], axis=0)), k, beta, e_neg)
    gram = each(lambda a, b, rt: _d3(sp(jnp.concatenate([a, b], axis=0)), rt, _NT), kk_d, r_d, right)
    v_s = each(sp, v)
    x = each(lambda g, a, vs: jnp.concatenate([a, _d3(sp(g[:c, :c] * strict), vs)], axis=1), gram, kk_d, v_s)
    pw = each(lambda g: sp(-(g[:c, c:] * strict)), gram)
    n = 1
    while True:
        x = each(lambda a, b: a + _d3(b, sp(a)), x, pw)
        n *= 2
        if n >= c:
            break
        pw = each(lambda b: sp(_d3(b, b)), pw)
    x_s = each(sp, x)
    mbx = each(lambda g, xs: _d3(sp(g[c:, c:] * incl), xs), gram, x_s)
    r1 = each(lambda a, b: a - b[:, :k_dim], r_d, mbx)
    y0 = each(lambda g, vs, b: _d3(sp(g[c:, :c] * incl), vs) - b[:, k_dim:], gram, v_s, mbx)
    xtb = each(lambda xs, rt: _d3(xs, (rt[0][c:], rt[1][c:]), _TN), x_s, right)
    eye = (lax.broadcasted_iota(I32, (k_dim, k_dim), 0)
           == lax.broadcasted_iota(I32, (k_dim, k_dim), 1)).astype(F32)
    p = each(lambda t, e: (eye - t[:k_dim]) * e[c - 1:c, :], xtb, e_in)
    q = each(lambda vs, rt, t, e: (_d3(vs, (rt[0][:c], rt[1][:c]), _TN) - t[k_dim:]) * e[c - 1:c, :],
             v_s, right, xtb, e_in)
    return r1, y0, p, q


def _rwkv_chunks_kernel(valid_ref, r_ref, k_ref, v_ref, kk_ref, b_ref, ld_ref,
                        r1_ref, y0_ref, p_ref, q_ref, *, n_chunks, n_heads):
    chunk = HS_B
    row = lax.broadcasted_iota(I32, (chunk, chunk), 0)
    col = lax.broadcasted_iota(I32, (chunk, chunk), 1)
    incl = (col <= row).astype(F32)
    strict = (col < row).astype(F32)
    incl_b = jnp.concatenate([incl, incl, incl], axis=1).astype(BF16)

    def body(ci, carry):
        rows = pl.ds(pl.multiple_of(ci * chunk, chunk), chunk)
        valid = valid_ref[rows, :]
        lanes = [slice(hh * HS_B, (hh + 1) * HS_B) for hh in range(n_heads)]
        heads = [(r_ref[rows, ln], k_ref[rows, ln] * valid, v_ref[rows, ln] * valid, kk_ref[rows, ln] * valid,
                  b_ref[rows, ln] * valid, ld_ref[rows, ln] * valid) for ln in lanes]
        for ln, r1, y0, p, q in zip(lanes, *_chunk_affine(heads, incl_b, incl, strict)):
            r1_ref[rows, ln] = r1
            y0_ref[rows, ln] = y0
            p_ref[rows, ln] = p
            q_ref[rows, ln] = q
        return carry

    lax.fori_loop(0, n_chunks, body, 0)


def rwkv_chunks(valid, r, k, v, kk, beta, ld):
    m, d_b = r.shape
    heads_per_step = 16
    width = heads_per_step * HS_B
    seq = pl.BlockSpec((ROW_TILE, width), lambda g, i: (i, g))
    return pl.pallas_call(
        functools.partial(_rwkv_chunks_kernel, n_chunks=ROW_TILE // HS_B, n_heads=heads_per_step),
        grid=(d_b // width, m // ROW_TILE),
        in_specs=[pl.BlockSpec((ROW_TILE, 1), lambda g, i: (i, 0))] + [seq] * 6,
        out_specs=[seq] * 4,
        out_shape=[jax.ShapeDtypeStruct((m, d_b), F32)] * 4,
        compiler_params=_params("parallel", "parallel"),
        name="rwkv_chunks",
    )(valid, r, k, v, kk, beta, ld)


def _rwkv_state_kernel(r1_ref, y0_ref, p_ref, q_ref, s0_ref, y_ref, s_ref, state_ref, *, n_chunks, n_heads):
    i = pl.program_id(0)
    chunk = HS_B

    @pl.when(i == 0)
    def _():
        state_ref[...] = s0_ref[...]

    def body(ci, carry):
        rows = pl.ds(pl.multiple_of(ci * chunk, chunk), chunk)
        lanes = [slice(hh * HS_B, (hh + 1) * HS_B) for hh in range(n_heads)]
        states = [_split2(state_ref[hh]) for hh in range(n_heads)]
        for hh, ln in enumerate(lanes):
            state_ref[hh] = _d3(states[hh], _split2(p_ref[rows, ln])) + q_ref[rows, ln]
        for hh, ln in enumerate(lanes):
            y_ref[rows, ln] = _d3(_split2(r1_ref[rows, ln]), states[hh], _NT) + y0_ref[rows, ln]
        return carry

    lax.fori_loop(0, n_chunks, body, 0)

    @pl.when(i == pl.num_programs(0) - 1)
    def _():
        s_ref[...] = state_ref[...]


def rwkv_state(r1, y0, p, q, s0):
    m, d_b = r1.shape
    n_heads = d_b // HS_B
    seq = pl.BlockSpec((ROW_TILE, d_b), lambda i: (i, 0))
    st = pl.BlockSpec((n_heads, HS_B, HS_B), lambda i: (0, 0, 0))
    return pl.pallas_call(
        functools.partial(_rwkv_state_kernel, n_chunks=ROW_TILE // HS_B, n_heads=n_heads),
        grid=(m // ROW_TILE,),
        in_specs=[seq] * 4 + [st],
        out_specs=[seq, st],
        out_shape=[jax.ShapeDtypeStruct((m, d_b), F32), jax.ShapeDtypeStruct(s0.shape, F32)],
        scratch_shapes=[pltpu.VMEM((n_heads, HS_B, HS_B), F32)],
        compiler_params=_params("arbitrary"),
        name="rwkv_state",
    )(r1, y0, p, q, s0)


def _rwkv_step_kernel(r_ref, k_ref, v_ref, kk_ref, b_ref, ld_ref, s0_ref, y_ref, s_ref, *, n_heads):
    eye = (lax.broadcasted_iota(I32, (HS_B, HS_B), 0) == lax.broadcasted_iota(I32, (HS_B, HS_B), 1))
    lanes = [slice(hh * HS_B, (hh + 1) * HS_B) for hh in range(n_heads)]
    row = lambda ref: [ref[0, :, ln] for ln in lanes]
    each = lambda f, *lists: [f(*args) for args in zip(*lists)]
    s0 = [s0_ref[0, hh] for hh in range(n_heads)]
    s_kk = each(lambda s, kk: jnp.sum(s * kk, axis=1, keepdims=True), s0, row(kk_ref))
    v_col = each(lambda v: jnp.sum(jnp.where(eye, v, 0.0), axis=1, keepdims=True), row(v_ref))
    s1 = each(lambda s, ld, sk, b, vc, k: s * jnp.exp(ld) - sk * b + vc * k,
              s0, row(ld_ref), s_kk, row(b_ref), v_col, row(k_ref))
    y_col = each(lambda s, r: jnp.sum(s * r, axis=1, keepdims=True), s1, row(r_ref))
    y_row = each(lambda y: jnp.sum(jnp.where(eye, y, 0.0), axis=0, keepdims=True), y_col)
    for hh, ln in enumerate(lanes):
        y_ref[0, :, ln] = y_row[hh]
        s_ref[0, hh] = s1[hh]


def rwkv_step(r, k, v, kk, beta, ld, s0):
    b, d_b = r.shape
    n_heads = d_b // HS_B
    row = pl.BlockSpec((1, 1, d_b), lambda i: (i, 0, 0))
    st = pl.BlockSpec((1, n_heads, HS_B, HS_B), lambda i: (i, 0, 0, 0))
    y, s = pl.pallas_call(
        functools.partial(_rwkv_step_kernel, n_heads=n_heads),
        grid=(b,),
        in_specs=[row] * 6 + [st],
        out_specs=[row, st],
        out_shape=[jax.ShapeDtypeStruct((b, 1, d_b), F32), jax.ShapeDtypeStruct(s0.shape, F32)],
        compiler_params=_params("parallel"),
        name="rwkv_step",
    )(*(a.reshape(b, 1, d_b) for a in (r, k, v, kk, beta, ld)), s0)
    return y.reshape(b, d_b), s


def _rwkv_post_kernel(y_ref, r_ref, k_ref, v_ref, g_ref, lng_ref, lnb_ref, rk_ref, seg_ref, segt_ref, o_ref):
    seg = seg_ref[...]
    seg_t = segt_ref[...]
    y = y_ref[...]
    inv_n = 1.0 / HS_B
    d = y - _seg_bcast(_seg_sum(y, seg) * inv_n, seg_t)
    var = _seg_sum(d * d, seg) * inv_n
    yn = d * _seg_bcast(lax.rsqrt(var + GN_EPS), seg_t) * lng_ref[...] + lnb_ref[...]
    bonus = _seg_bcast(_seg_sum(r_ref[...] * k_ref[...] * rk_ref[...], seg), seg_t) * v_ref[...]
    o_ref[...] = (yn + bonus) * g_ref[...]


def rwkv_post(y, r, k, v, g, ln_g, ln_b, r_k):
    m, d_b = y.shape
    seg, seg_t = _head_segments(d_b)
    row = pl.BlockSpec((ROW_TILE, d_b), lambda i: (i, 0))
    par = pl.BlockSpec((1, d_b), lambda i: (0, 0))
    return pl.pallas_call(
        _rwkv_post_kernel,
        grid=(m // ROW_TILE,),
        in_specs=[row] * 5 + [par] * 3 + [pl.BlockSpec((d_b, LANES), lambda i: (0, 0)),
                                          pl.BlockSpec((LANES, d_b), lambda i: (0, 0))],
        out_specs=row,
        out_shape=jax.ShapeDtypeStruct((m, d_b), F32),
        compiler_params=_params("parallel"),
        name="rwkv_post",
    )(y, r, k, v, g, ln_g.reshape(1, d_b), ln_b.reshape(1, d_b), r_k.reshape(1, d_b), seg, seg_t)


def _topk_rows(s, n_out, neg):
    n = s.shape[0]
    rows = lax.broadcasted_iota(I32, s.shape, 0).astype(F32)
    vals, idxs = [], []
    for _ in range(n_out):
        m = jnp.max(s, axis=0, keepdims=True)
        idx = jnp.min(jnp.where(s == m, rows, float(n)), axis=0, keepdims=True)
        s = jnp.where(rows == idx, neg, s)
        vals.append(m)
        idxs.append(idx)
    return vals, idxs


_CAND_LEN = [PEER_TOPK // (a + 1) for a in range(PEER_TOPK)]
_CAND_OFF = [sum(_CAND_LEN[:a]) for a in range(PEER_TOPK)]
_CAND_ROWS = -(-sum(_CAND_LEN) // 8) * 8


def _peer_topk_kernel(q_ref, keys_ref, e1_ref, e2_ref, gate_ref):
    neg = -jnp.inf
    dq = keys_ref.shape[-1]
    n_rows = q_ref.shape[0]
    e1_all, e2_all, gate_all = [], [], []
    for h in range(PEER_HEADS):
        tops = []
        for c in range(2):
            qc = q_ref[:, (2 * h + c) * dq:(2 * h + c + 1) * dq]
            s_t = _bdot_nt(keys_ref[h, c], qc)
            tops.append(_topk_rows(s_t, PEER_TOPK, neg))
        (v0, i0), (v1, i1) = tops
        cand = [v0[a] + v1[b] for a in range(PEER_TOPK) for b in range(_CAND_LEN[a])]
        cand.append(jnp.full((_CAND_ROWS - len(cand), n_rows), neg, F32))
        c_val, c_idx = _topk_rows(jnp.concatenate(cand, axis=0), PEER_TOPK, neg)
        pos = jnp.concatenate(c_idx, axis=0)
        a_sel = jnp.zeros_like(pos)
        off = jnp.zeros_like(pos)
        for a in range(1, PEER_TOPK):
            past = pos >= float(_CAND_OFF[a])
            a_sel = a_sel + jnp.where(past, 1.0, 0.0)
            off = off + jnp.where(past, float(_CAND_LEN[a - 1]), 0.0)
        b_sel = pos - off
        e1 = jnp.zeros_like(pos)
        e2 = jnp.zeros_like(pos)
        for a in range(PEER_TOPK):
            e1 = jnp.where(a_sel == float(a), i0[a], e1)
            e2 = jnp.where(b_sel == float(a), i1[a], e2)
        ex = jnp.exp(jnp.concatenate(c_val, axis=0) - c_val[0])
        e1_all.append(e1.astype(I32))
        e2_all.append(e2.astype(I32))
        gate_all.append(ex / jnp.sum(ex, axis=0, keepdims=True))
    e1_ref[...] = jnp.concatenate(e1_all, axis=0).T
    e2_ref[...] = jnp.concatenate(e2_all, axis=0).T
    gate_ref[...] = jnp.concatenate(gate_all, axis=0).T


def peer_topk(q, subkeys):
    m = q.shape[0]
    n_sel = PEER_HEADS * PEER_TOPK
    rows = LANES
    out = pl.BlockSpec((rows, n_sel), lambda i: (i, 0))
    return pl.pallas_call(
        _peer_topk_kernel,
        grid=(m // rows,),
        in_specs=[pl.BlockSpec((rows, q.shape[1]), lambda i: (i, 0)),
                  pl.BlockSpec(subkeys.shape, lambda i: (0, 0, 0, 0))],
        out_specs=[out, out, out],
        out_shape=[jax.ShapeDtypeStruct((m, n_sel), I32), jax.ShapeDtypeStruct((m, n_sel), I32),
                   jax.ShapeDtypeStruct((m, n_sel), F32)],
        compiler_params=_params("parallel"),
        name="peer_topk",
    )(q, subkeys)


def _peer_act_kernel(h_ref, u_ref, e1_ref, e2_ref, gate_ref, coef_ref, pre_ref):
    s = pl.program_id(1)

    @pl.when(s == 0)
    def _():
        pre_ref[...] = jnp.zeros_like(pre_ref)

    scores = lax.dot_general(h_ref[...], u_ref[...], _NT, preferred_element_type=F32)
    e1 = e1_ref[...]
    e2 = e2_ref[...]
    pre = pre_ref[...]
    for t in range(EXPERT_CHUNK // N_KEYS):
        picked = jnp.take_along_axis(scores[:, t * N_KEYS:(t + 1) * N_KEYS], e2, axis=1)
        pre = jnp.where(e1 == s * (EXPERT_CHUNK // N_KEYS) + t, picked, pre)
    pre_ref[...] = pre

    @pl.when(s == pl.num_programs(1) - 1)
    def _():
        coef_ref[...] = gate_ref[...] * jax.nn.gelu(pre)


def peer_act(hn, u, e1, e2, gate):
    m, d = hn.shape
    n_sel = e1.shape[1]
    rows = _row_tile(m)
    sel = pl.BlockSpec((rows, n_sel), lambda i, s: (i, 0))
    return pl.pallas_call(
        _peer_act_kernel,
        grid=(m // rows, u.shape[0] // EXPERT_CHUNK),
        in_specs=[pl.BlockSpec((rows, d), lambda i, s: (i, 0)),
                  pl.BlockSpec((EXPERT_CHUNK, d), lambda i, s: (s, 0)),
                  sel, sel, sel],
        out_specs=sel,
        out_shape=jax.ShapeDtypeStruct((m, n_sel), F32),
        scratch_shapes=[pltpu.VMEM((rows, n_sel), F32)],
        compiler_params=_params("parallel", "arbitrary"),
        name="peer_act",
    )(hn, u, e1, e2, gate)


def _peer_mix_kernel(x_ref, v_ref, e1_ref, e2_ref, coef_ref, o_ref, w_ref):
    s = pl.program_id(1)
    per_step = EXPERT_CHUNK // N_KEYS

    @pl.when(s == 0)
    def _():
        o_ref[...] = x_ref[...]
        n_sel = e1_ref.shape[1]
        key_iota = lax.broadcasted_iota(I32, (N_KEYS, n_sel), 0)

        def row_body(r, carry):
            e1 = e1_ref[pl.ds(r, 1), :]
            e2 = e2_ref[pl.ds(r, 1), :]
            cf = coef_ref[pl.ds(r, 1), :]
            a_t = jnp.where(key_iota == e1, cf, 0.0).astype(BF16)
            b_t = (key_iota == e2).astype(BF16)
            w_ref[pl.ds(pl.multiple_of(r * N_KEYS, N_KEYS), N_KEYS), :] = lax.dot_general(
                a_t, b_t, _NT, preferred_element_type=F32)
            return carry

        lax.fori_loop(0, x_ref.shape[0], row_body, 0, unroll=32)

    rows = x_ref.shape[0]
    tiles = [w_ref[pl.ds(s * per_step + t, rows, stride=N_KEYS), :].astype(BF16) for t in range(per_step)]
    o_ref[...] += jnp.dot(jnp.concatenate(tiles, axis=1), v_ref[...], preferred_element_type=F32)


def peer_mix(x, v, e1, e2, coef):
    m, d = x.shape
    n_sel = e1.shape[1]
    sel = pl.BlockSpec((ROW_TILE, n_sel), lambda i, s: (i, 0))
    return pl.pallas_call(
        _peer_mix_kernel,
        grid=(m // ROW_TILE, v.shape[0] // EXPERT_CHUNK),
        in_specs=[pl.BlockSpec((ROW_TILE, d), lambda i, s: (i, 0)),
                  pl.BlockSpec((EXPERT_CHUNK, d), lambda i, s: (s, 0)),
                  sel, sel, sel],
        out_specs=pl.BlockSpec((ROW_TILE, d), lambda i, s: (i, 0)),
        out_shape=jax.ShapeDtypeStruct((m, d), F32),
        scratch_shapes=[pltpu.VMEM((ROW_TILE * N_KEYS, N_KEYS), F32)],
        compiler_params=_params("parallel", "arbitrary"),
        name="peer_mix",
    )(x, v, e1, e2, coef)


def _layer(x_all, n_prompt, n_sample, cache_k, cache_v, page_table, wkv_s0, shift_s0, lw):
    m, d_model = x_all.shape
    d_a = lw["beta_a"].shape[0]
    d_b = lw["w0"].shape[0]
    n_heads_b = d_b // HS_B
    d_shift = lw["mu_shift"].shape[0]
    sample = slice(n_prompt, n_prompt + n_sample)

    hn = rmsnorm_rows(x_all, lw["norm_mix_g"])
    p = matmul_rows([hn], [lw["w_in"].astype(BF16)], tn=1280)
    qn, kn = qk_norm(p, lw["q_norm_g"], lw["k_norm_g"], d_a)
    va = p[:, 2 * d_a:3 * d_a]
    pb = p[:, 3 * d_a:]

    o_a = sb_attention_prompt(qn, kn, p, lw["sb_bias"], lw["beta_a"])
    o_s = sb_attention_sample(qn[sample], cache_k, cache_v, page_table, lw["sb_bias"], lw["beta_a"])
    o_a = lax.dynamic_update_slice(o_a, o_s, (n_prompt, 0))

    r, k, v, kk, beta, ld, g = rwkv_prep(p, 3 * d_a, shift_s0, n_prompt, lw)
    valid = (jnp.arange(m) < n_prompt).astype(F32)[:, None]
    y_b, wkv_p = rwkv_state(*rwkv_chunks(valid, r, k, v, kk, beta, ld), jnp.zeros((n_heads_b, HS_B, HS_B), F32))
    y_s, wkv_s = rwkv_step(*(a[sample] for a in (r, k, v, kk, beta, ld)), wkv_s0)
    y_b = lax.dynamic_update_slice(y_b, y_s, (n_prompt, 0))
    y_b = rwkv_post(y_b, r, k, v, g, lw["ln_x_g"], lw["ln_x_b"], lw["r_k"])

    w_out = lw["w_out"].astype(BF16)
    x1 = matmul_rows([o_a, y_b], [w_out[:d_a], w_out[d_a:]], res=x_all, tn=1024)

    hn2 = rmsnorm_rows(x1, lw["norm_ffn_g"])
    q = matmul_rows([hn2], [lw["peer_wq"].astype(BF16)], tn=1024)
    e1, e2, gate = peer_topk(q, lw["peer_subkeys"])
    coef = peer_act(hn2, lw["peer_u"].astype(BF16), e1, e2, gate)
    x2 = peer_mix(x1, lw["peer_v"].astype(BF16), e1, e2, coef)
    return x2, kn, va, wkv_p[None], wkv_s, pb


_LAYER_WEIGHTS = ("norm_mix_g", "w_in", "q_norm_g", "k_norm_g", "beta_a", "sb_bias", "mu_shift", "w0", "w_up",
                  "a0", "a_up", "g_up", "k_k", "k_a", "r_k", "ln_x_g", "ln_x_b", "w_out", "norm_ffn_g",
                  "peer_wq", "peer_subkeys", "peer_u", "peer_v")


def kernel(x_prompt, x_sample, cache_k, cache_v, page_table, state_wkv, state_shift, meta_tokens, norm_mix_g, w_in, q_norm_g, k_norm_g, beta_a, sb_bias, mu_shift, w0, w_up, a0, a_up, g_up, k_k, k_a, r_k, ln_x_g, ln_x_b, w_out, norm_ffn_g, peer_wq, peer_subkeys, peer_u, peer_v):
    weights = dict(zip(_LAYER_WEIGHTS, (norm_mix_g, w_in, q_norm_g, k_norm_g, beta_a, sb_bias, mu_shift, w0, w_up,
                                        a0, a_up, g_up, k_k, k_a, r_k, ln_x_g, ln_x_b, w_out, norm_ffn_g,
                                        peer_wq, peer_subkeys, peer_u, peer_v)))
    batch, seq, d_model = x_prompt.shape
    assert batch == 1 and x_sample.shape[1] == 1
    n_sample = x_sample.shape[0]
    n_prompt = seq + N_META
    depth = w_in.shape[0]
    m = -(-(n_prompt + n_sample) // ROW_TILE) * ROW_TILE
    x_all = jnp.concatenate([meta_tokens.astype(F32), x_prompt[0], x_sample[:, 0],
                             jnp.zeros((m - n_prompt - n_sample, d_model), F32)], axis=0)
    n_phys = cache_k.shape[1]
    d_a = beta_a.shape[1]
    n_heads_a = d_a // DH_A
    outs = [[] for _ in range(8)]
    for l in range(depth):
        lw = {name: w[l] for name, w in weights.items()}
        x_all, kn, va, wkv_p, wkv_s, pb = _layer(
            x_all, n_prompt, n_sample, cache_k[l].reshape(n_phys * PAGE_SIZE * n_heads_a, DH_A),
            cache_v[l].reshape(n_phys * PAGE_SIZE * n_heads_a, DH_A), page_table, state_wkv[l], state_shift[l], lw)
        heads = lambda a, rows: a[rows].reshape(-1, n_heads_a, DH_A)
        prompt = slice(0, n_prompt)
        sample = slice(n_prompt, n_prompt + n_sample)
        outs[0].append(heads(kn, prompt)[None])
        outs[1].append(heads(va, prompt)[None])
        outs[2].append(heads(kn, sample)[:, None])
        outs[3].append(heads(va, sample)[:, None])
        outs[4].append(wkv_p)
        outs[5].append(wkv_s)
        outs[6].append(pb[n_prompt - 1:n_prompt])
        outs[7].append(pb[sample])
    y_prompt = x_all[N_META:n_prompt][None]
    y_sample = x_all[n_prompt:n_prompt + n_sample][:, None]
    return (y_prompt, y_sample) + tuple(jnp.stack(o) for o in outs)
```

```python
import functools

import jax
import jax.numpy as jnp
from jax import lax
from jax.experimental import pallas as pl
from jax.experimental.pallas import tpu as pltpu

F32 = jnp.float32
BF16 = jnp.bfloat16
I32 = jnp.int32

LANES = 128
ROW_TILE = 256
VMEM_LIMIT = 56 * 1024 * 1024

N_META = 16
DH_A = 128
HS_B = 64
W_LORA = 64
A_LORA = 64
G_LORA = 128
N_KEYS = 128
PEER_HEADS = 8
PEER_TOPK = 16
PAGE_SIZE = 128
RMS_EPS = 1e-6
GN_EPS = 64e-5
L2_EPS = 1e-12
EXPERT_CHUNK = 2048
LOG2_E = 1.4426950408889634


def _row_tile(m, most=3):
    return next(k * ROW_TILE for k in range(most, 0, -1) if m % (k * ROW_TILE) == 0)


def _params(*sem):
    return pltpu.CompilerParams(dimension_semantics=sem, vmem_limit_bytes=VMEM_LIMIT)


_NN = (((1,), (0,)), ((), ()))
_NT = (((1,), (1,)), ((), ()))
_TN = (((0,), (0,)), ((), ()))


def _bdot(a, b):
    return jnp.dot(a.astype(BF16), b.astype(BF16), preferred_element_type=F32)


def _bdot_nt(a, b):
    return lax.dot_general(a.astype(BF16), b.astype(BF16), _NT, preferred_element_type=F32)


def _split2(x):
    hi = x.astype(BF16)
    lo = (x - hi.astype(F32)).astype(BF16)
    return hi, lo


def _split3(x):
    hi = x.astype(BF16)
    r1 = x - hi.astype(F32)
    mid = r1.astype(BF16)
    lo = (r1 - mid.astype(F32)).astype(BF16)
    return hi, mid, lo


def _rmsnorm_kernel(x_ref, g_ref, o_ref):
    x = x_ref[...]
    ms = jnp.mean(x * x, axis=-1, keepdims=True)
    o_ref[...] = (x * lax.rsqrt(ms + RMS_EPS) * g_ref[...]).astype(o_ref.dtype)


def rmsnorm_rows(x, g):
    m, d = x.shape
    return pl.pallas_call(
        _rmsnorm_kernel,
        grid=(m // ROW_TILE,),
        in_specs=[pl.BlockSpec((ROW_TILE, d), lambda i: (i, 0)),
                  pl.BlockSpec((1, d), lambda i: (0, 0))],
        out_specs=pl.BlockSpec((ROW_TILE, d), lambda i: (i, 0)),
        out_shape=jax.ShapeDtypeStruct((m, d), BF16),
        compiler_params=_params("parallel"),
        name="rmsnorm_rows",
    )(x, g.reshape(1, d))


def _mm_kernel(*refs, n_in, has_res):
    o_ref = refs[-1]
    acc = None
    for a_ref, w_ref in zip(refs[:n_in], refs[n_in:2 * n_in]):
        t = jnp.dot(a_ref[...].astype(BF16), w_ref[...], preferred_element_type=F32)
        acc = t if acc is None else acc + t
    if has_res:
        acc = acc + refs[2 * n_in][...]
    o_ref[...] = acc


def matmul_rows(a_list, w_list, res=None, *, tn):
    m = a_list[0].shape[0]
    n = w_list[0].shape[1]
    n_in = len(a_list)
    tm = _row_tile(m)
    in_specs = [pl.BlockSpec((tm, a.shape[1]), lambda j, i: (i, 0)) for a in a_list]
    in_specs += [pl.BlockSpec((w.shape[0], tn), lambda j, i: (0, j)) for w in w_list]
    args = list(a_list) + list(w_list)
    if res is not None:
        in_specs.append(pl.BlockSpec((tm, tn), lambda j, i: (i, j)))
        args.append(res)
    return pl.pallas_call(
        functools.partial(_mm_kernel, n_in=n_in, has_res=res is not None),
        grid=(n // tn, m // tm),
        in_specs=in_specs,
        out_specs=pl.BlockSpec((tm, tn), lambda j, i: (i, j)),
        out_shape=jax.ShapeDtypeStruct((m, n), F32),
        compiler_params=_params("parallel", "parallel"),
        name="matmul_rows",
    )(*args)


def _qknorm_kernel(p_ref, qg_ref, kg_ref, q_ref, k_ref, *, n_heads):
    for h in range(n_heads):
        for src, g_ref, dst in ((h, qg_ref, q_ref), (n_heads + h, kg_ref, k_ref)):
            x = p_ref[:, src * DH_A:(src + 1) * DH_A]
            ms = jnp.mean(x * x, axis=-1, keepdims=True)
            dst[:, h * DH_A:(h + 1) * DH_A] = x * lax.rsqrt(ms + RMS_EPS) * g_ref[...]


def qk_norm(p, q_g, k_g, d_a):
    m = p.shape[0]
    n_heads = d_a // DH_A
    return pl.pallas_call(
        functools.partial(_qknorm_kernel, n_heads=n_heads),
        grid=(m // ROW_TILE,),
        in_specs=[pl.BlockSpec((ROW_TILE, 2 * d_a), lambda i: (i, 0)),
                  pl.BlockSpec((1, DH_A), lambda i: (0, 0)),
                  pl.BlockSpec((1, DH_A), lambda i: (0, 0))],
        out_specs=[pl.BlockSpec((ROW_TILE, d_a), lambda i: (i, 0)),
                   pl.BlockSpec((ROW_TILE, d_a), lambda i: (i, 0))],
        out_shape=[jax.ShapeDtypeStruct((m, d_a), F32), jax.ShapeDtypeStruct((m, d_a), F32)],
        compiler_params=_params("parallel"),
        name="qk_norm",
    )(p, q_g.reshape(1, DH_A), k_g.reshape(1, DH_A))


def _strict_after(n):
    r = lax.broadcasted_iota(I32, (2 * n, n), 0) % n
    c = lax.broadcasted_iota(I32, (2 * n, n), 1)
    return (r > c).astype(BF16)


def _sb_logits(z2, after, mask):
    ls = jnp.minimum(z2, 0.0) - jnp.log(1.0 + jnp.exp2(-jnp.abs(z2))) * LOG2_E
    lk = ls - z2
    if mask is not None:
        lk = jnp.where(mask, lk, 0.0)
        ls = jnp.where(mask, ls, -jnp.inf)
    tail = jnp.dot(jnp.concatenate(_split2(lk), axis=1), after, preferred_element_type=F32)
    return ls, tail, tail[:, :1] + lk[:, :1]


def _sb_weights(ls, tail, run):
    return jnp.exp2(ls + tail + run).astype(BF16)


def _attn_kernel(bias_ref, q_ref, k_ref, v_ref, beta_ref, o_ref, *, blk):
    h = pl.program_id(0)
    i = pl.program_id(1)
    rows = q_ref.shape[0]
    q = (q_ref[...] * (DH_A ** -0.5 * LOG2_E)).astype(BF16)
    bias = bias_ref[h] * LOG2_E
    after = _strict_after(blk)

    def logits(j, mask):
        kb = k_ref[pl.ds(pl.multiple_of(j * blk, blk), blk), :].astype(BF16)
        return _sb_logits(lax.dot_general(q, kb, _NT, preferred_element_type=F32) + bias, after, mask)

    def values(j):
        return v_ref[pl.ds(pl.multiple_of(j * blk, blk), blk), :].astype(BF16)

    def key_blocks(js, acc, run, masks):
        parts = [logits(j, mask) for j, mask in zip(js, masks)]
        ws = []
        for ls, tail, lk_sum in parts:
            ws.append(_sb_weights(ls, tail, run))
            run = run + lk_sum
        for w, j in zip(ws, js):
            acc = acc + jnp.dot(w, values(j), preferred_element_type=F32)
        return acc, run

    nb = rows // blk
    q_pos = i * rows + lax.broadcasted_iota(I32, (rows, 1), 0)
    k_pos = lax.broadcasted_iota(I32, (1, blk), 1)
    top = [nb * i + b for b in reversed(range(nb))]
    acc, run = key_blocks(top, jnp.zeros((rows, DH_A), F32), jnp.zeros((rows, 1), F32),
                          [j * blk + k_pos < q_pos for j in top])

    def body(t, carry):
        m = i - 1 - t
        return key_blocks([nb * m + b for b in reversed(range(nb))], carry[0], carry[1], [None] * nb)

    acc, _ = lax.fori_loop(0, i, body, (acc, run))
    o_ref[...] = acc * beta_ref[...]


def sb_attention_prompt(q, k, p, bias, beta):
    t, d_a = q.shape
    n_heads = d_a // DH_A
    blk = ROW_TILE
    rows = _row_tile(t)
    return pl.pallas_call(
        functools.partial(_attn_kernel, blk=blk),
        grid=(n_heads, t // rows),
        in_specs=[pl.BlockSpec(memory_space=pltpu.SMEM),
                  pl.BlockSpec((rows, DH_A), lambda h, i: (i, h)),
                  pl.BlockSpec((t, DH_A), lambda h, i: (0, h)),
                  pl.BlockSpec((t, DH_A), lambda h, i: (0, 2 * n_heads + h)),
                  pl.BlockSpec((1, DH_A), lambda h, i: (0, h))],
        out_specs=pl.BlockSpec((rows, DH_A), lambda h, i: (i, h)),
        out_shape=jax.ShapeDtypeStruct((t, d_a), F32),
        compiler_params=_params("parallel", "parallel"),
        name="sb_attention_prompt",
    )(bias, q, k, p, beta.reshape(1, d_a))


def _attn_sample_kernel(pt_ref, q_ref, bias_ref, beta_ref, *refs, n_heads, n_pg):
    k_refs, v_refs = refs[:n_pg], refs[n_pg:2 * n_pg]
    o_ref, acc_ref, run_ref = refs[2 * n_pg:]
    j = pl.program_id(1)
    d_a = n_heads * DH_A
    head_of_lane = lax.broadcasted_iota(I32, (n_heads, d_a), 1) // DH_A
    own = head_of_lane == lax.broadcasted_iota(I32, (n_heads, d_a), 0)

    @pl.when(j == 0)
    def _():
        acc_ref[...] = jnp.zeros_like(acc_ref)
        run_ref[...] = jnp.zeros_like(run_ref)

    def page(ref):
        return jnp.concatenate([ref[pl.ds(hh, PAGE_SIZE, stride=n_heads), :] for hh in range(n_heads)],
                               axis=1).astype(BF16)

    q_rows = jnp.where(own, jnp.broadcast_to(q_ref[0] * (DH_A ** -0.5 * LOG2_E), (n_heads, d_a)), 0.0).astype(BF16)
    after = _strict_after(PAGE_SIZE)
    bias = bias_ref[...] * LOG2_E
    zs = [lax.dot_general(q_rows, page(k_ref), _NT, preferred_element_type=F32) + bias
          for k_ref in k_refs]
    parts = [_sb_logits(z, after, None) for z in zs]
    run = run_ref[...]
    ws = []
    for ls, tail, lk_sum in parts:
        ws.append(_sb_weights(ls, tail, run))
        run = run + lk_sum
    acc = acc_ref[...]
    for w, v_ref in zip(ws, v_refs):
        acc = acc + jnp.dot(w, page(v_ref), preferred_element_type=F32)
    acc_ref[...] = acc
    run_ref[...] = run

    @pl.when(j == pl.num_programs(1) - 1)
    def _():
        o = jnp.sum(jnp.where(own, acc, 0.0), axis=0, keepdims=True)
        o_ref[0] = o * beta_ref[...]


def sb_attention_sample(q, cache_k, cache_v, page_table, bias, beta):
    b, d_a = q.shape
    n_heads = d_a // DH_A
    n_pages = page_table.shape[1]
    n_pg = next(p for p in (8, 4, 2, 1) if n_pages % p == 0)

    def page_map(p):
        return lambda bi, j, pt: (pt[bi * n_pages + (n_pages - 1 - (j * n_pg + p))], 0)

    pages = [pl.BlockSpec((PAGE_SIZE * n_heads, DH_A), page_map(p)) for p in range(n_pg)]
    grid_spec = pltpu.PrefetchScalarGridSpec(
        num_scalar_prefetch=1,
        grid=(b, n_pages // n_pg),
        in_specs=[pl.BlockSpec((1, 1, d_a), lambda bi, j, pt: (bi, 0, 0)),
                  pl.BlockSpec((n_heads, 1), lambda bi, j, pt: (0, 0)),
                  pl.BlockSpec((1, d_a), lambda bi, j, pt: (0, 0))] + pages + pages,
        out_specs=pl.BlockSpec((1, 1, d_a), lambda bi, j, pt: (bi, 0, 0)),
        scratch_shapes=[pltpu.VMEM((n_heads, d_a), F32), pltpu.VMEM((n_heads, 1), F32)],
    )
    out = pl.pallas_call(
        functools.partial(_attn_sample_kernel, n_heads=n_heads, n_pg=n_pg),
        grid_spec=grid_spec,
        out_shape=jax.ShapeDtypeStruct((b, 1, d_a), F32),
        compiler_params=_params("parallel", "arbitrary"),
        name="sb_attention_sample",
    )(page_table.reshape(-1), q.reshape(b, 1, d_a), bias.reshape(n_heads, 1), beta.reshape(1, d_a),
      *([cache_k] * n_pg), *([cache_v] * n_pg))
    return out.reshape(b, d_a)


def _seg_sum(x, seg):
    hi, lo = _split2(x)
    return jnp.dot(hi, seg, preferred_element_type=F32) + jnp.dot(lo, seg, preferred_element_type=F32)


def _seg_bcast(y, seg_t):
    hi, mid, lo = _split3(y)
    return (jnp.dot(hi, seg_t, preferred_element_type=F32)
            + jnp.dot(mid, seg_t, preferred_element_type=F32)
            + jnp.dot(lo, seg_t, preferred_element_type=F32))


def _head_segments(d_b):
    assert d_b // HS_B <= LANES
    seg = (jnp.arange(d_b)[:, None] // HS_B == jnp.arange(LANES)[None, :]).astype(BF16)
    return seg, seg.T


def _rwkv_prep_kernel(p_ref, last_ref, shift_ref, mu_ref, w0_ref, a0_ref, kk_ref, ka_ref,
                      wup_ref, aup_ref, gup_ref, seg_ref, segt_ref,
                      r_ref, k_ref, v_ref, kn_ref, b_ref, ld_ref, g_ref, prev_ref,
                      *, d_b, col0, sample_block, sample_row):
    i = pl.program_id(0)
    d_shift = mu_ref.shape[1]
    p = p_ref[:, col0:col0 + d_shift]
    above = jnp.where(i == 0, 0.0, last_ref[last_ref.shape[0] - 1:, col0:col0 + d_shift])
    first = lax.broadcasted_iota(I32, (p.shape[0], 1), 0) == 0
    prev_ref[...] = jnp.where(first, above, pltpu.roll(p, 1, axis=0))

    @pl.when(i == sample_block)
    def _():
        prev_ref[pl.ds(sample_row, shift_ref.shape[0]), :] = shift_ref[...]

    xs = p + mu_ref[...] * (prev_ref[...] - p)
    r_ref[...] = xs[:, :d_b]
    k = xs[:, d_b:2 * d_b]
    v_ref[...] = xs[:, 2 * d_b:3 * d_b]
    o = 3 * d_b
    w_lo = xs[:, o:o + W_LORA]
    a_lo = xs[:, o + W_LORA:o + W_LORA + A_LORA]
    g_lo = xs[:, o + W_LORA + A_LORA:o + W_LORA + A_LORA + G_LORA]
    w = -jax.nn.softplus(-(w0_ref[...] + _bdot(jnp.tanh(w_lo), wup_ref[...]))) - 0.5
    ld_ref[...] = -jnp.exp(w)
    a = jax.nn.sigmoid(a0_ref[...] + _bdot(a_lo, aup_ref[...]))
    g_ref[...] = _bdot(jax.nn.sigmoid(g_lo), gup_ref[...])
    kk = k * kk_ref[...]
    inv = lax.rsqrt(_seg_sum(kk * kk, seg_ref[...]) + L2_EPS)
    kk = kk * _seg_bcast(inv, segt_ref[...])
    k_ref[...] = k * (1.0 + (a - 1.0) * ka_ref[...])
    kn_ref[...] = kk
    b_ref[...] = kk * a


def rwkv_prep(p, col0, shift_s, sample_start, lw):
    m, d_in = p.shape
    d_shift = lw["mu_shift"].shape[0]
    d_b = lw["w0"].shape[0]
    seg, seg_t = _head_segments(d_b)
    sample_block, sample_row = divmod(sample_start, ROW_TILE)
    assert sample_row + shift_s.shape[0] <= ROW_TILE and sample_row % 8 == 0
    sub = 8
    row = lambda n: pl.BlockSpec((ROW_TILE, n), lambda i: (i, 0))
    full = lambda a, b: pl.BlockSpec((a, b), lambda i: (0, 0))
    return pl.pallas_call(
        functools.partial(_rwkv_prep_kernel, d_b=d_b, col0=col0, sample_block=sample_block, sample_row=sample_row),
        grid=(m // ROW_TILE,),
        in_specs=[row(d_in),
                  pl.BlockSpec((sub, d_in), lambda i: (jnp.maximum(i * (ROW_TILE // sub) - 1, 0), 0)),
                  full(*shift_s.shape), full(1, d_shift), full(1, d_b), full(1, d_b),
                  full(1, d_b), full(1, d_b), full(W_LORA, d_b), full(A_LORA, d_b), full(G_LORA, d_b),
                  full(d_b, LANES), full(LANES, d_b)],
        out_specs=[row(d_b)] * 7,
        out_shape=[jax.ShapeDtypeStruct((m, d_b), F32)] * 7,
        scratch_shapes=[pltpu.VMEM((ROW_TILE, d_shift), F32)],
        compiler_params=_params("parallel"),
        name="rwkv_prep",
    )(p, p, shift_s, lw["mu_shift"].reshape(1, -1), lw["w0"].reshape(1, -1), lw["a0"].reshape(1, -1),
      lw["k_k"].reshape(1, -1), lw["k_a"].reshape(1, -1), lw["w_up"].astype(BF16), lw["a_up"].astype(BF16),
      lw["g_up"].astype(BF16), seg, seg_t)


def _d3(a, b, dims=_NN):
    if dims == _TN:
        return lax.dot_general(jnp.concatenate([a[0], a[0], a[1]], axis=0),
                               jnp.concatenate([b[0], b[1], b[0]], axis=0), dims, preferred_element_type=F32)
    dg = lambda x, y: lax.dot_general(x, y, dims, preferred_element_type=F32)
    m = a[0].shape[0]
    both = dg(jnp.concatenate(a, axis=0), b[0])
    return both[:m] + both[m:] + dg(a[0], b[1])


def _chunk_affine(heads, incl_b, incl, strict):
    c, k_dim = heads[0][0].shape
    sp = _split2
    each = lambda f, *lists: [f(*args) for args in zip(*lists)]
    r, k, v, kk, beta, ld = (list(col) for col in zip(*heads))
    cum = each(lambda x: jnp.dot(incl_b, jnp.concatenate(_split3(x), axis=0), preferred_element_type=F32), ld)
    e_in = each(jnp.exp, cum)
    e_neg = each(lambda x: jnp.exp(-x), cum)
    kk_d = each(lambda a, x, l: a * jnp.exp(x - l), kk, cum, ld)
    r_d = each(lambda a, e: a * e, r, e_in)
    right = each(lambda a, b, e: sp(jnp.concatenate([a * e, b * e], axis=0)), k, beta, e_neg)
    gram = each(lambda a, b, rt: _d3(sp(jnp.concatenate([a, b], axis=0)), rt, _NT), kk_d, r_d, right)
    v_s = each(sp, v)
    x = each(lambda g, a, vs: jnp.concatenate([a, _d3(sp(g[:c, :c] * strict), vs)], axis=1), gram, kk_d, v_s)
    pw = each(lambda g: sp(-(g[:c, c:] * strict)), gram)
    n = 1
    while True:
        x = each(lambda a, b: a + _d3(b, sp(a)), x, pw)
        n *= 2
        if n >= c:
            break
        pw = each(lambda b: sp(_d3(b, b)), pw)
    x_s = each(sp, x)
    mbx = each(lambda g, xs: _d3(sp(g[c:, c:] * incl), xs), gram, x_s)
    r1 = each(lambda a, b: a - b[:, :k_dim], r_d, mbx)
    y0 = each(lambda g, vs, b: _d3(sp(g[c:, :c] * incl), vs) - b[:, k_dim:], gram, v_s, mbx)
    xtb = each(lambda xs, rt: _d3(xs, (rt[0][c:], rt[1][c:]), _TN), x_s, right)
    eye = (lax.broadcasted_iota(I32, (k_dim, k_dim), 0)
           == lax.broadcasted_iota(I32, (k_dim, k_dim), 1)).astype(F32)
    p = each(lambda t, e: (eye - t[:k_dim]) * e[c - 1:c, :], xtb, e_in)
    q = each(lambda vs, rt, t, e: (_d3(vs, (rt[0][:c], rt[1][:c]), _TN) - t[k_dim:]) * e[c - 1:c, :],
             v_s, right, xtb, e_in)
    return r1, y0, p, q


def _rwkv_chunks_kernel(valid_ref, r_ref, k_ref, v_ref, kk_ref, b_ref, ld_ref,
                        r1_ref, y0_ref, p_ref, q_ref, *, n_chunks, n_heads):
    chunk = HS_B
    row = lax.broadcasted_iota(I32, (chunk, chunk), 0)
    col = lax.broadcasted_iota(I32, (chunk, chunk), 1)
    incl = (col <= row).astype(F32)
    strict = (col < row).astype(F32)
    incl_b = jnp.concatenate([incl, incl, incl], axis=1).astype(BF16)

    def body(ci, carry):
        rows = pl.ds(pl.multiple_of(ci * chunk, chunk), chunk)
        valid = valid_ref[rows, :]
        lanes = [slice(hh * HS_B, (hh + 1) * HS_B) for hh in range(n_heads)]
        heads = [(r_ref[rows, ln], k_ref[rows, ln] * valid, v_ref[rows, ln] * valid, kk_ref[rows, ln] * valid,
                  b_ref[rows, ln] * valid, ld_ref[rows, ln] * valid) for ln in lanes]
        for ln, r1, y0, p, q in zip(lanes, *_chunk_affine(heads, incl_b, incl, strict)):
            r1_ref[rows, ln] = r1
            y0_ref[rows, ln] = y0
            p_ref[rows, ln] = p
            q_ref[rows, ln] = q
        return carry

    lax.fori_loop(0, n_chunks, body, 0)


def rwkv_chunks(valid, r, k, v, kk, beta, ld):
    m, d_b = r.shape
    heads_per_step = 16
    width = heads_per_step * HS_B
    seq = pl.BlockSpec((ROW_TILE, width), lambda g, i: (i, g))
    return pl.pallas_call(
        functools.partial(_rwkv_chunks_kernel, n_chunks=ROW_TILE // HS_B, n_heads=heads_per_step),
        grid=(d_b // width, m // ROW_TILE),
        in_specs=[pl.BlockSpec((ROW_TILE, 1), lambda g, i: (i, 0))] + [seq] * 6,
        out_specs=[seq] * 4,
        out_shape=[jax.ShapeDtypeStruct((m, d_b), F32)] * 4,
        compiler_params=_params("parallel", "parallel"),
        name="rwkv_chunks",
    )(valid, r, k, v, kk, beta, ld)


def _rwkv_state_kernel(r1_ref, y0_ref, p_ref, q_ref, s0_ref, y_ref, s_ref, state_ref, *, n_chunks, n_heads):
    i = pl.program_id(0)
    chunk = HS_B

    @pl.when(i == 0)
    def _():
        state_ref[...] = s0_ref[...]

    def body(ci, carry):
        rows = pl.ds(pl.multiple_of(ci * chunk, chunk), chunk)
        lanes = [slice(hh * HS_B, (hh + 1) * HS_B) for hh in range(n_heads)]
        states = [_split2(state_ref[hh]) for hh in range(n_heads)]
        for hh, ln in enumerate(lanes):
            state_ref[hh] = _d3(states[hh], _split2(p_ref[rows, ln])) + q_ref[rows, ln]
        for hh, ln in enumerate(lanes):
            y_ref[rows, ln] = _d3(_split2(r1_ref[rows, ln]), states[hh], _NT) + y0_ref[rows, ln]
        return carry

    lax.fori_loop(0, n_chunks, body, 0)

    @pl.when(i == pl.num_programs(0) - 1)
    def _():
        s_ref[...] = state_ref[...]


def rwkv_state(r1, y0, p, q, s0):
    m, d_b = r1.shape
    n_heads = d_b // HS_B
    seq = pl.BlockSpec((ROW_TILE, d_b), lambda i: (i, 0))
    st = pl.BlockSpec((n_heads, HS_B, HS_B), lambda i: (0, 0, 0))
    return pl.pallas_call(
        functools.partial(_rwkv_state_kernel, n_chunks=ROW_TILE // HS_B, n_heads=n_heads),
        grid=(m // ROW_TILE,),
        in_specs=[seq] * 4 + [st],
        out_specs=[seq, st],
        out_shape=[jax.ShapeDtypeStruct((m, d_b), F32), jax.ShapeDtypeStruct(s0.shape, F32)],
        scratch_shapes=[pltpu.VMEM((n_heads, HS_B, HS_B), F32)],
        compiler_params=_params("arbitrary"),
        name="rwkv_state",
    )(r1, y0, p, q, s0)


def _rwkv_step_kernel(r_ref, k_ref, v_ref, kk_ref, b_ref, ld_ref, s0_ref, y_ref, s_ref, *, n_heads):
    eye = (lax.broadcasted_iota(I32, (HS_B, HS_B), 0) == lax.broadcasted_iota(I32, (HS_B, HS_B), 1))
    lanes = [slice(hh * HS_B, (hh + 1) * HS_B) for hh in range(n_heads)]
    row = lambda ref: [ref[0, :, ln] for ln in lanes]
    each = lambda f, *lists: [f(*args) for args in zip(*lists)]
    s0 = [s0_ref[0, hh] for hh in range(n_heads)]
    s_kk = each(lambda s, kk: jnp.sum(s * kk, axis=1, keepdims=True), s0, row(kk_ref))
    v_col = each(lambda v: jnp.sum(jnp.where(eye, v, 0.0), axis=1, keepdims=True), row(v_ref))
    s1 = each(lambda s, ld, sk, b, vc, k: s * jnp.exp(ld) - sk * b + vc * k,
              s0, row(ld_ref), s_kk, row(b_ref), v_col, row(k_ref))
    y_col = each(lambda s, r: jnp.sum(s * r, axis=1, keepdims=True), s1, row(r_ref))
    y_row = each(lambda y: jnp.sum(jnp.where(eye, y, 0.0), axis=0, keepdims=True), y_col)
    for hh, ln in enumerate(lanes):
        y_ref[0, :, ln] = y_row[hh]
        s_ref[0, hh] = s1[hh]


def rwkv_step(r, k, v, kk, beta, ld, s0):
    b, d_b = r.shape
    n_heads = d_b // HS_B
    row = pl.BlockSpec((1, 1, d_b), lambda i: (i, 0, 0))
    st = pl.BlockSpec((1, n_heads, HS_B, HS_B), lambda i: (i, 0, 0, 0))
    y, s = pl.pallas_call(
        functools.partial(_rwkv_step_kernel, n_heads=n_heads),
        grid=(b,),
        in_specs=[row] * 6 + [st],
        out_specs=[row, st],
        out_shape=[jax.ShapeDtypeStruct((b, 1, d_b), F32), jax.ShapeDtypeStruct(s0.shape, F32)],
        compiler_params=_params("parallel"),
        name="rwkv_step",
    )(*(a.reshape(b, 1, d_b) for a in (r, k, v, kk, beta, ld)), s0)
    return y.reshape(b, d_b), s


def _rwkv_post_kernel(y_ref, r_ref, k_ref, v_ref, g_ref, lng_ref, lnb_ref, rk_ref, seg_ref, segt_ref, o_ref):
    seg = seg_ref[...]
    seg_t = segt_ref[...]
    y = y_ref[...]
    inv_n = 1.0 / HS_B
    d = y - _seg_bcast(_seg_sum(y, seg) * inv_n, seg_t)
    var = _seg_sum(d * d, seg) * inv_n
    yn = d * _seg_bcast(lax.rsqrt(var + GN_EPS), seg_t) * lng_ref[...] + lnb_ref[...]
    bonus = _seg_bcast(_seg_sum(r_ref[...] * k_ref[...] * rk_ref[...], seg), seg_t) * v_ref[...]
    o_ref[...] = (yn + bonus) * g_ref[...]


def rwkv_post(y, r, k, v, g, ln_g, ln_b, r_k):
    m, d_b = y.shape
    seg, seg_t = _head_segments(d_b)
    row = pl.BlockSpec((ROW_TILE, d_b), lambda i: (i, 0))
    par = pl.BlockSpec((1, d_b), lambda i: (0, 0))
    return pl.pallas_call(
        _rwkv_post_kernel,
        grid=(m // ROW_TILE,),
        in_specs=[row] * 5 + [par] * 3 + [pl.BlockSpec((d_b, LANES), lambda i: (0, 0)),
                                          pl.BlockSpec((LANES, d_b), lambda i: (0, 0))],
        out_specs=row,
        out_shape=jax.ShapeDtypeStruct((m, d_b), F32),
        compiler_params=_params("parallel"),
        name="rwkv_post",
    )(y, r, k, v, g, ln_g.reshape(1, d_b), ln_b.reshape(1, d_b), r_k.reshape(1, d_b), seg, seg_t)


def _topk_rows(s, n_out, neg):
    n = s.shape[0]
    rows = lax.broadcasted_iota(I32, s.shape, 0).astype(F32)
    vals, idxs = [], []
    for _ in range(n_out):
        m = jnp.max(s, axis=0, keepdims=True)
        idx = jnp.min(jnp.where(s == m, rows, float(n)), axis=0, keepdims=True)
        s = jnp.where(rows == idx, neg, s)
        vals.append(m)
        idxs.append(idx)
    return vals, idxs


_CAND_LEN = [PEER_TOPK // (a + 1) for a in range(PEER_TOPK)]
_CAND_OFF = [sum(_CAND_LEN[:a]) for a in range(PEER_TOPK)]
_CAND_ROWS = -(-sum(_CAND_LEN) // 8) * 8


def _peer_topk_kernel(q_ref, keys_ref, e1_ref, e2_ref, gate_ref):
    neg = -jnp.inf
    dq = keys_ref.shape[-1]
    n_rows = q_ref.shape[0]
    e1_all, e2_all, gate_all = [], [], []
    for h in range(PEER_HEADS):
        tops = []
        for c in range(2):
            qc = q_ref[:, (2 * h + c) * dq:(2 * h + c + 1) * dq]
            s_t = _bdot_nt(keys_ref[h, c], qc)
            tops.append(_topk_rows(s_t, PEER_TOPK, neg))
        (v0, i0), (v1, i1) = tops
        cand = [v0[a] + v1[b] for a in range(PEER_TOPK) for b in range(_CAND_LEN[a])]
        cand.append(jnp.full((_CAND_ROWS - len(cand), n_rows), neg, F32))
        c_val, c_idx = _topk_rows(jnp.concatenate(cand, axis=0), PEER_TOPK, neg)
        pos = jnp.concatenate(c_idx, axis=0)
        a_sel = jnp.zeros_like(pos)
        off = jnp.zeros_like(pos)
        for a in range(1, PEER_TOPK):
            past = pos >= float(_CAND_OFF[a])
            a_sel = a_sel + jnp.where(past, 1.0, 0.0)
            off = off + jnp.where(past, float(_CAND_LEN[a - 1]), 0.0)
        b_sel = pos - off
        e1 = jnp.zeros_like(pos)
        e2 = jnp.zeros_like(pos)
        for a in range(PEER_TOPK):
            e1 = jnp.where(a_sel == float(a), i0[a], e1)
            e2 = jnp.where(b_sel == float(a), i1[a], e2)
        ex = jnp.exp(jnp.concatenate(c_val, axis=0) - c_val[0])
        e1_all.append(e1.astype(I32))
        e2_all.append(e2.astype(I32))
        gate_all.append(ex / jnp.sum(ex, axis=0, keepdims=True))
    e1_ref[...] = jnp.concatenate(e1_all, axis=0).T
    e2_ref[...] = jnp.concatenate(e2_all, axis=0).T
    gate_ref[...] = jnp.concatenate(gate_all, axis=0).T


def peer_topk(q, subkeys):
    m = q.shape[0]
    n_sel = PEER_HEADS * PEER_TOPK
    rows = LANES
    out = pl.BlockSpec((rows, n_sel), lambda i: (i, 0))
    return pl.pallas_call(
        _peer_topk_kernel,
        grid=(m // rows,),
        in_specs=[pl.BlockSpec((rows, q.shape[1]), lambda i: (i, 0)),
                  pl.BlockSpec(subkeys.shape, lambda i: (0, 0, 0, 0))],
        out_specs=[out, out, out],
        out_shape=[jax.ShapeDtypeStruct((m, n_sel), I32), jax.ShapeDtypeStruct((m, n_sel), I32),
                   jax.ShapeDtypeStruct((m, n_sel), F32)],
        compiler_params=_params("parallel"),
        name="peer_topk",
    )(q, subkeys)


def _peer_act_kernel(h_ref, u_ref, e1_ref, e2_ref, gate_ref, coef_ref, pre_ref):
    s = pl.program_id(1)

    @pl.when(s == 0)
    def _():
        pre_ref[...] = jnp.zeros_like(pre_ref)

    scores = lax.dot_general(h_ref[...], u_ref[...], _NT, preferred_element_type=F32)
    e1 = e1_ref[...]
    e2 = e2_ref[...]
    pre = pre_ref[...]
    for t in range(EXPERT_CHUNK // N_KEYS):
        picked = jnp.take_along_axis(scores[:, t * N_KEYS:(t + 1) * N_KEYS], e2, axis=1)
        pre = jnp.where(e1 == s * (EXPERT_CHUNK // N_KEYS) + t, picked, pre)
    pre_ref[...] = pre

    @pl.when(s == pl.num_programs(1) - 1)
    def _():
        coef_ref[...] = gate_ref[...] * jax.nn.gelu(pre)


def peer_act(hn, u, e1, e2, gate):
    m, d = hn.shape
    n_sel = e1.shape[1]
    rows = _row_tile(m)
    sel = pl.BlockSpec((rows, n_sel), lambda i, s: (i, 0))
    return pl.pallas_call(
        _peer_act_kernel,
        grid=(m // rows, u.shape[0] // EXPERT_CHUNK),
        in_specs=[pl.BlockSpec((rows, d), lambda i, s: (i, 0)),
                  pl.BlockSpec((EXPERT_CHUNK, d), lambda i, s: (s, 0)),
                  sel, sel, sel],
        out_specs=sel,
        out_shape=jax.ShapeDtypeStruct((m, n_sel), F32),
        scratch_shapes=[pltpu.VMEM((rows, n_sel), F32)],
        compiler_params=_params("parallel", "arbitrary"),
        name="peer_act",
    )(hn, u, e1, e2, gate)


def _peer_mix_kernel(x_ref, v_ref, e1_ref, e2_ref, coef_ref, o_ref, w_ref):
    s = pl.program_id(1)
    per_step = EXPERT_CHUNK // N_KEYS

    @pl.when(s == 0)
    def _():
        o_ref[...] = x_ref[...]
        n_sel = e1_ref.shape[1]
        key_iota = lax.broadcasted_iota(I32, (N_KEYS, n_sel), 0)

        def row_body(r, carry):
            e1 = e1_ref[pl.ds(r, 1), :]
            e2 = e2_ref[pl.ds(r, 1), :]
            cf = coef_ref[pl.ds(r, 1), :]
            a_t = jnp.where(key_iota == e1, cf, 0.0).astype(BF16)
            b_t = (key_iota == e2).astype(BF16)
            w_ref[pl.ds(pl.multiple_of(r * N_KEYS, N_KEYS), N_KEYS), :] = lax.dot_general(
                a_t, b_t, _NT, preferred_element_type=F32)
            return carry

        lax.fori_loop(0, x_ref.shape[0], row_body, 0, unroll=64)

    rows = x_ref.shape[0]
    tiles = [w_ref[pl.ds(s * per_step + t, rows, stride=N_KEYS), :].astype(BF16) for t in range(per_step)]
    o_ref[...] += jnp.dot(jnp.concatenate(tiles, axis=1), v_ref[...], preferred_element_type=F32)


def peer_mix(x, v, e1, e2, coef):
    m, d = x.shape
    n_sel = e1.shape[1]
    rows = next(r for r in (3 * ROW_TILE // 2, ROW_TILE) if m % r == 0)
    sel = pl.BlockSpec((rows, n_sel), lambda i, s: (i, 0))
    return pl.pallas_call(
        _peer_mix_kernel,
        grid=(m // rows, v.shape[0] // EXPERT_CHUNK),
        in_specs=[pl.BlockSpec((rows, d), lambda i, s: (i, 0)),
                  pl.BlockSpec((EXPERT_CHUNK, d), lambda i, s: (s, 0)),
                  sel, sel, sel],
        out_specs=pl.BlockSpec((rows, d), lambda i, s: (i, 0)),
        out_shape=jax.ShapeDtypeStruct((m, d), F32),
        scratch_shapes=[pltpu.VMEM((rows * N_KEYS, N_KEYS), F32)],
        compiler_params=_params("parallel", "arbitrary"),
        name="peer_mix",
    )(x, v, e1, e2, coef)


def _layer(x_all, n_prompt, n_sample, cache_k, cache_v, page_table, wkv_s0, shift_s0, lw):
    m, d_model = x_all.shape
    d_a = lw["beta_a"].shape[0]
    d_b = lw["w0"].shape[0]
    n_heads_b = d_b // HS_B
    d_shift = lw["mu_shift"].shape[0]
    sample = slice(n_prompt, n_prompt + n_sample)

    hn = rmsnorm_rows(x_all, lw["norm_mix_g"])
    p = matmul_rows([hn], [lw["w_in"].astype(BF16)], tn=1280)
    qn, kn = qk_norm(p, lw["q_norm_g"], lw["k_norm_g"], d_a)
    va = p[:, 2 * d_a:3 * d_a]
    pb = p[:, 3 * d_a:]

    o_a = sb_attention_prompt(qn, kn, p, lw["sb_bias"], lw["beta_a"])
    o_s = sb_attention_sample(qn[sample], cache_k, cache_v, page_table, lw["sb_bias"], lw["beta_a"])
    o_a = lax.dynamic_update_slice(o_a, o_s, (n_prompt, 0))

    r, k, v, kk, beta, ld, g = rwkv_prep(p, 3 * d_a, shift_s0, n_prompt, lw)
    valid = (jnp.arange(m) < n_prompt).astype(F32)[:, None]
    y_b, wkv_p = rwkv_state(*rwkv_chunks(valid, r, k, v, kk, beta, ld), jnp.zeros((n_heads_b, HS_B, HS_B), F32))
    y_s, wkv_s = rwkv_step(*(a[sample] for a in (r, k, v, kk, beta, ld)), wkv_s0)
    y_b = lax.dynamic_update_slice(y_b, y_s, (n_prompt, 0))
    y_b = rwkv_post(y_b, r, k, v, g, lw["ln_x_g"], lw["ln_x_b"], lw["r_k"])

    w_out = lw["w_out"].astype(BF16)
    x1 = matmul_rows([o_a, y_b], [w_out[:d_a], w_out[d_a:]], res=x_all, tn=1024)

    hn2 = rmsnorm_rows(x1, lw["norm_ffn_g"])
    q = matmul_rows([hn2], [lw["peer_wq"].astype(BF16)], tn=1024)
    e1, e2, gate = peer_topk(q, lw["peer_subkeys"])
    coef = peer_act(hn2, lw["peer_u"].astype(BF16), e1, e2, gate)
    x2 = peer_mix(x1, lw["peer_v"].astype(BF16), e1, e2, coef)
    return x2, kn, va, wkv_p[None], wkv_s, pb


_LAYER_WEIGHTS = ("norm_mix_g", "w_in", "q_norm_g", "k_norm_g", "beta_a", "sb_bias", "mu_shift", "w0", "w_up",
                  "a0", "a_up", "g_up", "k_k", "k_a", "r_k", "ln_x_g", "ln_x_b", "w_out", "norm_ffn_g",
                  "peer_wq", "peer_subkeys", "peer_u", "peer_v")


def kernel(x_prompt, x_sample, cache_k, cache_v, page_table, state_wkv, state_shift, meta_tokens, norm_mix_g, w_in, q_norm_g, k_norm_g, beta_a, sb_bias, mu_shift, w0, w_up, a0, a_up, g_up, k_k, k_a, r_k, ln_x_g, ln_x_b, w_out, norm_ffn_g, peer_wq, peer_subkeys, peer_u, peer_v):
    weights = dict(zip(_LAYER_WEIGHTS, (norm_mix_g, w_in, q_norm_g, k_norm_g, beta_a, sb_bias, mu_shift, w0, w_up,
                                        a0, a_up, g_up, k_k, k_a, r_k, ln_x_g, ln_x_b, w_out, norm_ffn_g,
                                        peer_wq, peer_subkeys, peer_u, peer_v)))
    batch, seq, d_model = x_prompt.shape
    assert batch == 1 and x_sample.shape[1] == 1
    n_sample = x_sample.shape[0]
    n_prompt = seq + N_META
    depth = w_in.shape[0]
    m = -(-(n_prompt + n_sample) // ROW_TILE) * ROW_TILE
    x_all = jnp.concatenate([meta_tokens.astype(F32), x_prompt[0], x_sample[:, 0],
                             jnp.zeros((m - n_prompt - n_sample, d_model), F32)], axis=0)
    n_phys = cache_k.shape[1]
    d_a = beta_a.shape[1]
    n_heads_a = d_a // DH_A
    outs = [[] for _ in range(8)]
    for l in range(depth):
        lw = {name: w[l] for name, w in weights.items()}
        x_all, kn, va, wkv_p, wkv_s, pb = _layer(
            x_all, n_prompt, n_sample, cache_k[l].reshape(n_phys * PAGE_SIZE * n_heads_a, DH_A),
            cache_v[l].reshape(n_phys * PAGE_SIZE * n_heads_a, DH_A), page_table, state_wkv[l], state_shift[l], lw)
        heads = lambda a, rows: a[rows].reshape(-1, n_heads_a, DH_A)
        prompt = slice(0, n_prompt)
        sample = slice(n_prompt, n_prompt + n_sample)
        outs[0].append(heads(kn, prompt)[None])
        outs[1].append(heads(va, prompt)[None])
        outs[2].append(heads(kn, sample)[:, None])
        outs[3].append(heads(va, sample)[:, None])
        outs[4].append(wkv_p)
        outs[5].append(wkv_s)
        outs[6].append(pb[n_prompt - 1:n_prompt])
        outs[7].append(pb[sample])
    y_prompt = x_all[N_META:n_prompt][None]
    y_sample = x_all[n_prompt:n_prompt + n_sample][:, None]
    return (y_prompt, y_sample) + tuple(jnp.stack(o) for o in outs)
```

```python
import functools

import jax
import jax.numpy as jnp
from jax import lax
from jax.experimental import pallas as pl
from jax.experimental.pallas import tpu as pltpu

F32 = jnp.float32
BF16 = jnp.bfloat16
I32 = jnp.int32

LANES = 128
ROW_TILE = 256
VMEM_LIMIT = 56 * 1024 * 1024

N_META = 16
DH_A = 128
HS_B = 64
W_LORA = 64
A_LORA = 64
G_LORA = 128
N_KEYS = 128
PEER_HEADS = 8
PEER_TOPK = 16
PAGE_SIZE = 128
RMS_EPS = 1e-6
GN_EPS = 64e-5
L2_EPS = 1e-12
EXPERT_CHUNK = 2048
LOG2_E = 1.4426950408889634


def _row_tile(m, most=3):
    return next(k * ROW_TILE for k in range(most, 0, -1) if m % (k * ROW_TILE) == 0)


def _params(*sem):
    return pltpu.CompilerParams(dimension_semantics=sem, vmem_limit_bytes=VMEM_LIMIT)


_NN = (((1,), (0,)), ((), ()))
_NT = (((1,), (1,)), ((), ()))
_TN = (((0,), (0,)), ((), ()))


def _bdot(a, b):
    return jnp.dot(a.astype(BF16), b.astype(BF16), preferred_element_type=F32)


def _bdot_nt(a, b):
    return lax.dot_general(a.astype(BF16), b.astype(BF16), _NT, preferred_element_type=F32)


def _split2(x):
    hi = x.astype(BF16)
    lo = (x - hi.astype(F32)).astype(BF16)
    return hi, lo


def _split3(x):
    hi = x.astype(BF16)
    r1 = x - hi.astype(F32)
    mid = r1.astype(BF16)
    lo = (r1 - mid.astype(F32)).astype(BF16)
    return hi, mid, lo


def _rmsnorm_kernel(x_ref, g_ref, o_ref):
    x = x_ref[...]
    ms = jnp.mean(x * x, axis=-1, keepdims=True)
    o_ref[...] = (x * lax.rsqrt(ms + RMS_EPS) * g_ref[...]).astype(o_ref.dtype)


def rmsnorm_rows(x, g):
    m, d = x.shape
    return pl.pallas_call(
        _rmsnorm_kernel,
        grid=(m // ROW_TILE,),
        in_specs=[pl.BlockSpec((ROW_TILE, d), lambda i: (i, 0)),
                  pl.BlockSpec((1, d), lambda i: (0, 0))],
        out_specs=pl.BlockSpec((ROW_TILE, d), lambda i: (i, 0)),
        out_shape=jax.ShapeDtypeStruct((m, d), BF16),
        compiler_params=_params("parallel"),
        name="rmsnorm_rows",
    )(x, g.reshape(1, d))


def _mm_kernel(*refs, n_in, has_res):
    o_ref = refs[-1]
    acc = None
    for a_ref, w_ref in zip(refs[:n_in], refs[n_in:2 * n_in]):
        t = jnp.dot(a_ref[...].astype(BF16), w_ref[...], preferred_element_type=F32)
        acc = t if acc is None else acc + t
    if has_res:
        acc = acc + refs[2 * n_in][...]
    o_ref[...] = acc


def matmul_rows(a_list, w_list, res=None, *, tn):
    m = a_list[0].shape[0]
    n = w_list[0].shape[1]
    n_in = len(a_list)
    tm = _row_tile(m)
    in_specs = [pl.BlockSpec((tm, a.shape[1]), lambda j, i: (i, 0)) for a in a_list]
    in_specs += [pl.BlockSpec((w.shape[0], tn), lambda j, i: (0, j)) for w in w_list]
    args = list(a_list) + list(w_list)
    if res is not None:
        in_specs.append(pl.BlockSpec((tm, tn), lambda j, i: (i, j)))
        args.append(res)
    return pl.pallas_call(
        functools.partial(_mm_kernel, n_in=n_in, has_res=res is not None),
        grid=(n // tn, m // tm),
        in_specs=in_specs,
        out_specs=pl.BlockSpec((tm, tn), lambda j, i: (i, j)),
        out_shape=jax.ShapeDtypeStruct((m, n), F32),
        compiler_params=_params("parallel", "parallel"),
        name="matmul_rows",
    )(*args)


def _qknorm_kernel(p_ref, qg_ref, kg_ref, q_ref, k_ref, *, n_heads):
    for h in range(n_heads):
        for src, g_ref, dst in ((h, qg_ref, q_ref), (n_heads + h, kg_ref, k_ref)):
            x = p_ref[:, src * DH_A:(src + 1) * DH_A]
            ms = jnp.mean(x * x, axis=-1, keepdims=True)
            dst[:, h * DH_A:(h + 1) * DH_A] = x * lax.rsqrt(ms + RMS_EPS) * g_ref[...]


def qk_norm(p, q_g, k_g, d_a):
    m = p.shape[0]
    n_heads = d_a // DH_A
    return pl.pallas_call(
        functools.partial(_qknorm_kernel, n_heads=n_heads),
        grid=(m // ROW_TILE,),
        in_specs=[pl.BlockSpec((ROW_TILE, 2 * d_a), lambda i: (i, 0)),
                  pl.BlockSpec((1, DH_A), lambda i: (0, 0)),
                  pl.BlockSpec((1, DH_A), lambda i: (0, 0))],
        out_specs=[pl.BlockSpec((ROW_TILE, d_a), lambda i: (i, 0)),
                   pl.BlockSpec((ROW_TILE, d_a), lambda i: (i, 0))],
        out_shape=[jax.ShapeDtypeStruct((m, d_a), F32), jax.ShapeDtypeStruct((m, d_a), F32)],
        compiler_params=_params("parallel"),
        name="qk_norm",
    )(p, q_g.reshape(1, DH_A), k_g.reshape(1, DH_A))


def _strict_after(n):
    r = lax.broadcasted_iota(I32, (2 * n, n), 0) % n
    c = lax.broadcasted_iota(I32, (2 * n, n), 1)
    return (r > c).astype(BF16)


def _sb_logits(z2, after, mask):
    ls = jnp.minimum(z2, 0.0) - jnp.log(1.0 + jnp.exp2(-jnp.abs(z2))) * LOG2_E
    lk = ls - z2
    if mask is not None:
        lk = jnp.where(mask, lk, 0.0)
        ls = jnp.where(mask, ls, -jnp.inf)
    tail = jnp.dot(jnp.concatenate(_split2(lk), axis=1), after, preferred_element_type=F32)
    return ls, tail, tail[:, :1] + lk[:, :1]


def _sb_weights(ls, tail, run):
    return jnp.exp2(ls + tail + run).astype(BF16)


def _attn_kernel(bias_ref, q_ref, k_ref, v_ref, beta_ref, o_ref, *, blk):
    h = pl.program_id(0)
    i = pl.program_id(1)
    rows = q_ref.shape[0]
    q = (q_ref[...] * (DH_A ** -0.5 * LOG2_E)).astype(BF16)
    bias = bias_ref[h] * LOG2_E
    after = _strict_after(blk)

    def logits(qq, j, mask):
        kb = k_ref[pl.ds(pl.multiple_of(j * blk, blk), blk), :].astype(BF16)
        return _sb_logits(lax.dot_general(qq, kb, _NT, preferred_element_type=F32) + bias, after, mask)

    def values(j):
        return v_ref[pl.ds(pl.multiple_of(j * blk, blk), blk), :].astype(BF16)

    def key_blocks(js, acc, run):
        parts = [logits(q, j, None) for j in js]
        ws = []
        for ls, tail, lk_sum in parts:
            ws.append(_sb_weights(ls, tail, run))
            run = run + lk_sum
        for w, j in zip(ws, js):
            acc = acc + jnp.dot(w, values(j), preferred_element_type=F32)
        return acc, run

    nb = rows // blk
    q_pos = i * rows + lax.broadcasted_iota(I32, (rows, 1), 0)
    k_pos = lax.broadcasted_iota(I32, (1, blk), 1)
    top = [(b, nb * i + b) for b in reversed(range(nb))]
    parts = [logits(q[b * blk:], j, j * blk + k_pos < q_pos[b * blk:]) for b, j in top]
    run = [jnp.zeros((blk, 1), F32)] * nb
    acc = [jnp.zeros((blk, DH_A), F32)] * nb
    ws = []
    for (b, j), (ls, tail, lk_sum) in zip(top, parts):
        ws.append(_sb_weights(ls, tail, jnp.concatenate(run[b:], axis=0)))
        for a in range(b, nb):
            run[a] = run[a] + lk_sum[(a - b) * blk:(a - b + 1) * blk]
    for (b, j), w in zip(top, ws):
        pv = jnp.dot(w, values(j), preferred_element_type=F32)
        for a in range(b, nb):
            acc[a] = acc[a] + pv[(a - b) * blk:(a - b + 1) * blk]
    acc = jnp.concatenate(acc, axis=0)
    run = jnp.concatenate(run, axis=0)

    def body(t, carry):
        m = i - 1 - t
        return key_blocks([nb * m + b for b in reversed(range(nb))], carry[0], carry[1])

    acc, _ = lax.fori_loop(0, i, body, (acc, run))
    o_ref[...] = acc * beta_ref[...]


def sb_attention_prompt(q, k, p, bias, beta):
    t, d_a = q.shape
    n_heads = d_a // DH_A
    blk = ROW_TILE
    rows = _row_tile(t)
    return pl.pallas_call(
        functools.partial(_attn_kernel, blk=blk),
        grid=(n_heads, t // rows),
        in_specs=[pl.BlockSpec(memory_space=pltpu.SMEM),
                  pl.BlockSpec((rows, DH_A), lambda h, i: (i, h)),
                  pl.BlockSpec((t, DH_A), lambda h, i: (0, h)),
                  pl.BlockSpec((t, DH_A), lambda h, i: (0, 2 * n_heads + h)),
                  pl.BlockSpec((1, DH_A), lambda h, i: (0, h))],
        out_specs=pl.BlockSpec((rows, DH_A), lambda h, i: (i, h)),
        out_shape=jax.ShapeDtypeStruct((t, d_a), F32),
        compiler_params=_params("parallel", "parallel"),
        name="sb_attention_prompt",
    )(bias, q, k, p, beta.reshape(1, d_a))


def _attn_sample_kernel(pt_ref, q_ref, bias_ref, beta_ref, *refs, n_heads, n_pg):
    k_refs, v_refs = refs[:n_pg], refs[n_pg:2 * n_pg]
    o_ref, acc_ref, run_ref = refs[2 * n_pg:]
    j = pl.program_id(1)
    d_a = n_heads * DH_A
    head_of_lane = lax.broadcasted_iota(I32, (n_heads, d_a), 1) // DH_A
    own = head_of_lane == lax.broadcasted_iota(I32, (n_heads, d_a), 0)

    @pl.when(j == 0)
    def _():
        acc_ref[...] = jnp.zeros_like(acc_ref)
        run_ref[...] = jnp.zeros_like(run_ref)

    def page(ref):
        return jnp.concatenate([ref[pl.ds(hh, PAGE_SIZE, stride=n_heads), :] for hh in range(n_heads)],
                               axis=1).astype(BF16)

    q_rows = jnp.where(own, jnp.broadcast_to(q_ref[0] * (DH_A ** -0.5 * LOG2_E), (n_heads, d_a)), 0.0).astype(BF16)
    after = _strict_after(PAGE_SIZE)
    bias = bias_ref[...] * LOG2_E
    zs = [lax.dot_general(q_rows, page(k_ref), _NT, preferred_element_type=F32) + bias
          for k_ref in k_refs]
    parts = [_sb_logits(z, after, None) for z in zs]
    run = run_ref[...]
    ws = []
    for ls, tail, lk_sum in parts:
        ws.append(_sb_weights(ls, tail, run))
        run = run + lk_sum
    acc = acc_ref[...]
    for w, v_ref in zip(ws, v_refs):
        acc = acc + jnp.dot(w, page(v_ref), preferred_element_type=F32)
    acc_ref[...] = acc
    run_ref[...] = run

    @pl.when(j == pl.num_programs(1) - 1)
    def _():
        o = jnp.sum(jnp.where(own, acc, 0.0), axis=0, keepdims=True)
        o_ref[0] = o * beta_ref[...]


def sb_attention_sample(q, cache_k, cache_v, page_table, bias, beta):
    b, d_a = q.shape
    n_heads = d_a // DH_A
    n_pages = page_table.shape[1]
    n_pg = next(p for p in (8, 4, 2, 1) if n_pages % p == 0)

    def page_map(p):
        return lambda bi, j, pt: (pt[bi * n_pages + (n_pages - 1 - (j * n_pg + p))], 0)

    pages = [pl.BlockSpec((PAGE_SIZE * n_heads, DH_A), page_map(p)) for p in range(n_pg)]
    grid_spec = pltpu.PrefetchScalarGridSpec(
        num_scalar_prefetch=1,
        grid=(b, n_pages // n_pg),
        in_specs=[pl.BlockSpec((1, 1, d_a), lambda bi, j, pt: (bi, 0, 0)),
                  pl.BlockSpec((n_heads, 1), lambda bi, j, pt: (0, 0)),
                  pl.BlockSpec((1, d_a), lambda bi, j, pt: (0, 0))] + pages + pages,
        out_specs=pl.BlockSpec((1, 1, d_a), lambda bi, j, pt: (bi, 0, 0)),
        scratch_shapes=[pltpu.VMEM((n_heads, d_a), F32), pltpu.VMEM((n_heads, 1), F32)],
    )
    out = pl.pallas_call(
        functools.partial(_attn_sample_kernel, n_heads=n_heads, n_pg=n_pg),
        grid_spec=grid_spec,
        out_shape=jax.ShapeDtypeStruct((b, 1, d_a), F32),
        compiler_params=_params("parallel", "arbitrary"),
        name="sb_attention_sample",
    )(page_table.reshape(-1), q.reshape(b, 1, d_a), bias.reshape(n_heads, 1), beta.reshape(1, d_a),
      *([cache_k] * n_pg), *([cache_v] * n_pg))
    return out.reshape(b, d_a)


def _seg_sum(x, seg):
    hi, lo = _split2(x)
    return jnp.dot(hi, seg, preferred_element_type=F32) + jnp.dot(lo, seg, preferred_element_type=F32)


def _seg_bcast(y, seg_t):
    hi, mid, lo = _split3(y)
    return (jnp.dot(hi, seg_t, preferred_element_type=F32)
            + jnp.dot(mid, seg_t, preferred_element_type=F32)
            + jnp.dot(lo, seg_t, preferred_element_type=F32))


def _head_segments(d_b):
    assert d_b // HS_B <= LANES
    seg = (jnp.arange(d_b)[:, None] // HS_B == jnp.arange(LANES)[None, :]).astype(BF16)
    return seg, seg.T


def _rwkv_prep_kernel(p_ref, last_ref, shift_ref, mu_ref, w0_ref, a0_ref, kk_ref, ka_ref,
                      wup_ref, aup_ref, gup_ref, seg_ref, segt_ref,
                      r_ref, k_ref, v_ref, kn_ref, b_ref, ld_ref, g_ref, prev_ref,
                      *, d_b, col0, sample_block, sample_row):
    i = pl.program_id(0)
    d_shift = mu_ref.shape[1]
    p = p_ref[:, col0:col0 + d_shift]
    above = jnp.where(i == 0, 0.0, last_ref[last_ref.shape[0] - 1:, col0:col0 + d_shift])
    first = lax.broadcasted_iota(I32, (p.shape[0], 1), 0) == 0
    prev_ref[...] = jnp.where(first, above, pltpu.roll(p, 1, axis=0))

    @pl.when(i == sample_block)
    def _():
        prev_ref[pl.ds(sample_row, shift_ref.shape[0]), :] = shift_ref[...]

    xs = p + mu_ref[...] * (prev_ref[...] - p)
    r_ref[...] = xs[:, :d_b]
    k = xs[:, d_b:2 * d_b]
    v_ref[...] = xs[:, 2 * d_b:3 * d_b]
    o = 3 * d_b
    w_lo = xs[:, o:o + W_LORA]
    a_lo = xs[:, o + W_LORA:o + W_LORA + A_LORA]
    g_lo = xs[:, o + W_LORA + A_LORA:o + W_LORA + A_LORA + G_LORA]
    w = -jax.nn.softplus(-(w0_ref[...] + _bdot(jnp.tanh(w_lo), wup_ref[...]))) - 0.5
    ld_ref[...] = -jnp.exp(w)
    a = jax.nn.sigmoid(a0_ref[...] + _bdot(a_lo, aup_ref[...]))
    g_ref[...] = _bdot(jax.nn.sigmoid(g_lo), gup_ref[...])
    kk = k * kk_ref[...]
    inv = lax.rsqrt(_seg_sum(kk * kk, seg_ref[...]) + L2_EPS)
    kk = kk * _seg_bcast(inv, segt_ref[...])
    k_ref[...] = k * (1.0 + (a - 1.0) * ka_ref[...])
    kn_ref[...] = kk
    b_ref[...] = kk * a


def rwkv_prep(p, col0, shift_s, sample_start, lw):
    m, d_in = p.shape
    d_shift = lw["mu_shift"].shape[0]
    d_b = lw["w0"].shape[0]
    seg, seg_t = _head_segments(d_b)
    sample_block, sample_row = divmod(sample_start, ROW_TILE)
    assert sample_row + shift_s.shape[0] <= ROW_TILE and sample_row % 8 == 0
    sub = 8
    row = lambda n: pl.BlockSpec((ROW_TILE, n), lambda i: (i, 0))
    full = lambda a, b: pl.BlockSpec((a, b), lambda i: (0, 0))
    return pl.pallas_call(
        functools.partial(_rwkv_prep_kernel, d_b=d_b, col0=col0, sample_block=sample_block, sample_row=sample_row),
        grid=(m // ROW_TILE,),
        in_specs=[row(d_in),
                  pl.BlockSpec((sub, d_in), lambda i: (jnp.maximum(i * (ROW_TILE // sub) - 1, 0), 0)),
                  full(*shift_s.shape), full(1, d_shift), full(1, d_b), full(1, d_b),
                  full(1, d_b), full(1, d_b), full(W_LORA, d_b), full(A_LORA, d_b), full(G_LORA, d_b),
                  full(d_b, LANES), full(LANES, d_b)],
        out_specs=[row(d_b)] * 7,
        out_shape=[jax.ShapeDtypeStruct((m, d_b), F32)] * 7,
        scratch_shapes=[pltpu.VMEM((ROW_TILE, d_shift), F32)],
        compiler_params=_params("parallel"),
        name="rwkv_prep",
    )(p, p, shift_s, lw["mu_shift"].reshape(1, -1), lw["w0"].reshape(1, -1), lw["a0"].reshape(1, -1),
      lw["k_k"].reshape(1, -1), lw["k_a"].reshape(1, -1), lw["w_up"].astype(BF16), lw["a_up"].astype(BF16),
      lw["g_up"].astype(BF16), seg, seg_t)


def _d3(a, b, dims=_NN):
    if dims == _TN:
        return lax.dot_general(jnp.concatenate([a[0], a[0], a[1]], axis=0),
                               jnp.concatenate([b[0], b[1], b[0]], axis=0), dims, preferred_element_type=F32)
    dg = lambda x, y: lax.dot_general(x, y, dims, preferred_element_type=F32)
    m = a[0].shape[0]
    both = dg(jnp.concatenate(a, axis=0), b[0])
    return both[:m] + both[m:] + dg(a[0], b[1])


def _chunk_affine(heads, incl_b, incl, strict):
    c, k_dim = heads[0][0].shape
    sp = _split2
    each = lambda f, *lists: [f(*args) for args in zip(*lists)]
    r, k, v, kk, beta, ld = (list(col) for col in zip(*heads))
    cum = each(lambda x: jnp.dot(incl_b, jnp.concatenate(_split3(x), axis=0), preferred_element_type=F32), ld)
    e_in = each(jnp.exp, cum)
    e_neg = each(lambda x: jnp.exp(-x), cum)
    kk_d = each(lambda a, x, l: a * jnp.exp(x - l), kk, cum, ld)
    r_d = each(lambda a, e: a * e, r, e_in)
    right = each(lambda a, b, e: sp(jnp.concatenate([a * e, b * e], axis=0)), k, beta, e_neg)
    gram = each(lambda a, b, rt: _d3(sp(jnp.concatenate([a, b], axis=0)), rt, _NT), kk_d, r_d, right)
    v_s = each(sp, v)
    x = each(lambda g, a, vs: jnp.concatenate([a, _d3(sp(g[:c, :c] * strict), vs)], axis=1), gram, kk_d, v_s)
    pw = each(lambda g: sp(-(g[:c, c:] * strict)), gram)
    n = 1
    while True:
        x = each(lambda a, b: a + _d3(b, sp(a)), x, pw)
        n *= 2
        if n >= c:
            break
        pw = each(lambda b: sp(_d3(b, b)), pw)
    x_s = each(sp, x)
    mbx = each(lambda g, xs: _d3(sp(g[c:, c:] * incl), xs), gram, x_s)
    r1 = each(lambda a, b: a - b[:, :k_dim], r_d, mbx)
    y0 = each(lambda g, vs, b: _d3(sp(g[c:, :c] * incl), vs) - b[:, k_dim:], gram, v_s, mbx)
    xtb = each(lambda xs, rt: _d3(xs, (rt[0][c:], rt[1][c:]), _TN), x_s, right)
    eye = (lax.broadcasted_iota(I32, (k_dim, k_dim), 0)
           == lax.broadcasted_iota(I32, (k_dim, k_dim), 1)).astype(F32)
    p = each(lambda t, e: (eye - t[:k_dim]) * e[c - 1:c, :], xtb, e_in)
    q = each(lambda vs, rt, t, e: (_d3(vs, (rt[0][:c], rt[1][:c]), _TN) - t[k_dim:]) * e[c - 1:c, :],
             v_s, right, xtb, e_in)
    return r1, y0, p, q


def _rwkv_chunks_kernel(valid_ref, r_ref, k_ref, v_ref, kk_ref, b_ref, ld_ref,
                        r1_ref, y0_ref, p_ref, q_ref, *, n_chunks, n_heads):
    chunk = HS_B
    row = lax.broadcasted_iota(I32, (chunk, chunk), 0)
    col = lax.broadcasted_iota(I32, (chunk, chunk), 1)
    incl = (col <= row).astype(F32)
    strict = (col < row).astype(F32)
    incl_b = jnp.concatenate([incl, incl, incl], axis=1).astype(BF16)

    def body(ci, carry):
        rows = pl.ds(pl.multiple_of(ci * chunk, chunk), chunk)
        valid = valid_ref[rows, :]
        lanes = [slice(hh * HS_B, (hh + 1) * HS_B) for hh in range(n_heads)]
        heads = [(r_ref[rows, ln], k_ref[rows, ln] * valid, v_ref[rows, ln] * valid, kk_ref[rows, ln] * valid,
                  b_ref[rows, ln] * valid, ld_ref[rows, ln] * valid) for ln in lanes]
        for ln, r1, y0, p, q in zip(lanes, *_chunk_affine(heads, incl_b, incl, strict)):
            r1_ref[rows, ln] = r1
            y0_ref[rows, ln] = y0
            p_ref[rows, ln] = p
            q_ref[rows, ln] = q
        return carry

    lax.fori_loop(0, n_chunks, body, 0)


def rwkv_chunks(valid, r, k, v, kk, beta, ld):
    m, d_b = r.shape
    heads_per_step = 16
    width = heads_per_step * HS_B
    seq = pl.BlockSpec((ROW_TILE, width), lambda g, i: (i, g))
    return pl.pallas_call(
        functools.partial(_rwkv_chunks_kernel, n_chunks=ROW_TILE // HS_B, n_heads=heads_per_step),
        grid=(d_b // width, m // ROW_TILE),
        in_specs=[pl.BlockSpec((ROW_TILE, 1), lambda g, i: (i, 0))] + [seq] * 6,
        out_specs=[seq] * 4,
        out_shape=[jax.ShapeDtypeStruct((m, d_b), F32)] * 4,
        compiler_params=_params("parallel", "parallel"),
        name="rwkv_chunks",
    )(valid, r, k, v, kk, beta, ld)


def _rwkv_state_kernel(r1_ref, y0_ref, p_ref, q_ref, s0_ref, y_ref, s_ref, state_ref, *, n_chunks, n_heads):
    i = pl.program_id(0)
    chunk = HS_B

    @pl.when(i == 0)
    def _():
        state_ref[...] = s0_ref[...]

    def body(ci, carry):
        rows = pl.ds(pl.multiple_of(ci * chunk, chunk), chunk)
        lanes = [slice(hh * HS_B, (hh + 1) * HS_B) for hh in range(n_heads)]
        states = [_split2(state_ref[hh]) for hh in range(n_heads)]
        for hh, ln in enumerate(lanes):
            state_ref[hh] = _d3(states[hh], _split2(p_ref[rows, ln])) + q_ref[rows, ln]
        for hh, ln in enumerate(lanes):
            y_ref[rows, ln] = _d3(_split2(r1_ref[rows, ln]), states[hh], _NT) + y0_ref[rows, ln]
        return carry

    lax.fori_loop(0, n_chunks, body, 0)

    @pl.when(i == pl.num_programs(0) - 1)
    def _():
        s_ref[...] = state_ref[...]


def rwkv_state(r1, y0, p, q, s0):
    m, d_b = r1.shape
    n_heads = d_b // HS_B
    seq = pl.BlockSpec((ROW_TILE, d_b), lambda i: (i, 0))
    st = pl.BlockSpec((n_heads, HS_B, HS_B), lambda i: (0, 0, 0))
    return pl.pallas_call(
        functools.partial(_rwkv_state_kernel, n_chunks=ROW_TILE // HS_B, n_heads=n_heads),
        grid=(m // ROW_TILE,),
        in_specs=[seq] * 4 + [st],
        out_specs=[seq, st],
        out_shape=[jax.ShapeDtypeStruct((m, d_b), F32), jax.ShapeDtypeStruct(s0.shape, F32)],
        scratch_shapes=[pltpu.VMEM((n_heads, HS_B, HS_B), F32)],
        compiler_params=_params("arbitrary"),
        name="rwkv_state",
    )(r1, y0, p, q, s0)


def _rwkv_step_kernel(r_ref, k_ref, v_ref, kk_ref, b_ref, ld_ref, s0_ref, y_ref, s_ref, *, n_heads):
    eye = (lax.broadcasted_iota(I32, (HS_B, HS_B), 0) == lax.broadcasted_iota(I32, (HS_B, HS_B), 1))
    lanes = [slice(hh * HS_B, (hh + 1) * HS_B) for hh in range(n_heads)]
    row = lambda ref: [ref[0, :, ln] for ln in lanes]
    each = lambda f, *lists: [f(*args) for args in zip(*lists)]
    s0 = [s0_ref[0, hh] for hh in range(n_heads)]
    s_kk = each(lambda s, kk: jnp.sum(s * kk, axis=1, keepdims=True), s0, row(kk_ref))
    v_col = each(lambda v: jnp.sum(jnp.where(eye, v, 0.0), axis=1, keepdims=True), row(v_ref))
    s1 = each(lambda s, ld, sk, b, vc, k: s * jnp.exp(ld) - sk * b + vc * k,
              s0, row(ld_ref), s_kk, row(b_ref), v_col, row(k_ref))
    y_col = each(lambda s, r: jnp.sum(s * r, axis=1, keepdims=True), s1, row(r_ref))
    y_row = each(lambda y: jnp.sum(jnp.where(eye, y, 0.0), axis=0, keepdims=True), y_col)
    for hh, ln in enumerate(lanes):
        y_ref[0, :, ln] = y_row[hh]
        s_ref[0, hh] = s1[hh]


def rwkv_step(r, k, v, kk, beta, ld, s0):
    b, d_b = r.shape
    n_heads = d_b // HS_B
    row = pl.BlockSpec((1, 1, d_b), lambda i: (i, 0, 0))
    st = pl.BlockSpec((1, n_heads, HS_B, HS_B), lambda i: (i, 0, 0, 0))
    y, s = pl.pallas_call(
        functools.partial(_rwkv_step_kernel, n_heads=n_heads),
        grid=(b,),
        in_specs=[row] * 6 + [st],
        out_specs=[row, st],
        out_shape=[jax.ShapeDtypeStruct((b, 1, d_b), F32), jax.ShapeDtypeStruct(s0.shape, F32)],
        compiler_params=_params("parallel"),
        name="rwkv_step",
    )(*(a.reshape(b, 1, d_b) for a in (r, k, v, kk, beta, ld)), s0)
    return y.reshape(b, d_b), s


def _rwkv_post_kernel(y_ref, r_ref, k_ref, v_ref, g_ref, lng_ref, lnb_ref, rk_ref, seg_ref, segt_ref, o_ref):
    seg = seg_ref[...]
    seg_t = segt_ref[...]
    y = y_ref[...]
    inv_n = 1.0 / HS_B
    d = y - _seg_bcast(_seg_sum(y, seg) * inv_n, seg_t)
    var = _seg_sum(d * d, seg) * inv_n
    yn = d * _seg_bcast(lax.rsqrt(var + GN_EPS), seg_t) * lng_ref[...] + lnb_ref[...]
    bonus = _seg_bcast(_seg_sum(r_ref[...] * k_ref[...] * rk_ref[...], seg), seg_t) * v_ref[...]
    o_ref[...] = (yn + bonus) * g_ref[...]


def rwkv_post(y, r, k, v, g, ln_g, ln_b, r_k):
    m, d_b = y.shape
    seg, seg_t = _head_segments(d_b)
    row = pl.BlockSpec((ROW_TILE, d_b), lambda i: (i, 0))
    par = pl.BlockSpec((1, d_b), lambda i: (0, 0))
    return pl.pallas_call(
        _rwkv_post_kernel,
        grid=(m // ROW_TILE,),
        in_specs=[row] * 5 + [par] * 3 + [pl.BlockSpec((d_b, LANES), lambda i: (0, 0)),
                                          pl.BlockSpec((LANES, d_b), lambda i: (0, 0))],
        out_specs=row,
        out_shape=jax.ShapeDtypeStruct((m, d_b), F32),
        compiler_params=_params("parallel"),
        name="rwkv_post",
    )(y, r, k, v, g, ln_g.reshape(1, d_b), ln_b.reshape(1, d_b), r_k.reshape(1, d_b), seg, seg_t)


def _topk_rows(s, n_out, neg):
    n = s.shape[0]
    rows = lax.broadcasted_iota(I32, s.shape, 0).astype(F32)
    vals, idxs = [], []
    for _ in range(n_out):
        m = jnp.max(s, axis=0, keepdims=True)
        idx = jnp.min(jnp.where(s == m, rows, float(n)), axis=0, keepdims=True)
        s = jnp.where(rows == idx, neg, s)
        vals.append(m)
        idxs.append(idx)
    return vals, idxs


_CAND_LEN = [PEER_TOPK // (a + 1) for a in range(PEER_TOPK)]
_CAND_OFF = [sum(_CAND_LEN[:a]) for a in range(PEER_TOPK)]
_CAND_ROWS = -(-sum(_CAND_LEN) // 8) * 8


def _peer_topk_kernel(q_ref, keys_ref, e1_ref, e2_ref, gate_ref):
    neg = -jnp.inf
    dq = keys_ref.shape[-1]
    n_rows = q_ref.shape[0]
    e1_all, e2_all, gate_all = [], [], []
    for h in range(PEER_HEADS):
        tops = []
        for c in range(2):
            qc = q_ref[:, (2 * h + c) * dq:(2 * h + c + 1) * dq]
            s_t = _bdot_nt(keys_ref[h, c], qc)
            tops.append(_topk_rows(s_t, PEER_TOPK, neg))
        (v0, i0), (v1, i1) = tops
        cand = [v0[a] + v1[b] for a in range(PEER_TOPK) for b in range(_CAND_LEN[a])]
        cand.append(jnp.full((_CAND_ROWS - len(cand), n_rows), neg, F32))
        c_val, c_idx = _topk_rows(jnp.concatenate(cand, axis=0), PEER_TOPK, neg)
        pos = jnp.concatenate(c_idx, axis=0)
        a_sel = jnp.zeros_like(pos)
        off = jnp.zeros_like(pos)
        for a in range(1, PEER_TOPK):
            past = pos >= float(_CAND_OFF[a])
            a_sel = a_sel + jnp.where(past, 1.0, 0.0)
            off = off + jnp.where(past, float(_CAND_LEN[a - 1]), 0.0)
        b_sel = pos - off
        e1 = jnp.zeros_like(pos)
        e2 = jnp.zeros_like(pos)
        for a in range(PEER_TOPK):
            e1 = jnp.where(a_sel == float(a), i0[a], e1)
            e2 = jnp.where(b_sel == float(a), i1[a], e2)
        ex = jnp.exp(jnp.concatenate(c_val, axis=0) - c_val[0])
        e1_all.append(e1.astype(I32))
        e2_all.append(e2.astype(I32))
        gate_all.append(ex / jnp.sum(ex, axis=0, keepdims=True))
    e1_ref[...] = jnp.concatenate(e1_all, axis=0).T
    e2_ref[...] = jnp.concatenate(e2_all, axis=0).T
    gate_ref[...] = jnp.concatenate(gate_all, axis=0).T


def peer_topk(q, subkeys):
    m = q.shape[0]
    n_sel = PEER_HEADS * PEER_TOPK
    rows = LANES
    out = pl.BlockSpec((rows, n_sel), lambda i: (i, 0))
    return pl.pallas_call(
        _peer_topk_kernel,
        grid=(m // rows,),
        in_specs=[pl.BlockSpec((rows, q.shape[1]), lambda i: (i, 0)),
                  pl.BlockSpec(subkeys.shape, lambda i: (0, 0, 0, 0))],
        out_specs=[out, out, out],
        out_shape=[jax.ShapeDtypeStruct((m, n_sel), I32), jax.ShapeDtypeStruct((m, n_sel), I32),
                   jax.ShapeDtypeStruct((m, n_sel), F32)],
        compiler_params=_params("parallel"),
        name="peer_topk",
    )(q, subkeys)


def _peer_act_kernel(h_ref, u_ref, e1_ref, e2_ref, gate_ref, coef_ref, pre_ref):
    s = pl.program_id(1)

    @pl.when(s == 0)
    def _():
        pre_ref[...] = jnp.zeros_like(pre_ref)

    scores = lax.dot_general(h_ref[...], u_ref[...], _NT, preferred_element_type=F32)
    e1 = e1_ref[...]
    e2 = e2_ref[...]
    pre = pre_ref[...]
    for t in range(EXPERT_CHUNK // N_KEYS):
        picked = jnp.take_along_axis(scores[:, t * N_KEYS:(t + 1) * N_KEYS], e2, axis=1)
        pre = jnp.where(e1 == s * (EXPERT_CHUNK // N_KEYS) + t, picked, pre)
    pre_ref[...] = pre

    @pl.when(s == pl.num_programs(1) - 1)
    def _():
        coef_ref[...] = gate_ref[...] * jax.nn.gelu(pre)


def peer_act(hn, u, e1, e2, gate):
    m, d = hn.shape
    n_sel = e1.shape[1]
    rows = _row_tile(m)
    sel = pl.BlockSpec((rows, n_sel), lambda i, s: (i, 0))
    return pl.pallas_call(
        _peer_act_kernel,
        grid=(m // rows, u.shape[0] // EXPERT_CHUNK),
        in_specs=[pl.BlockSpec((rows, d), lambda i, s: (i, 0)),
                  pl.BlockSpec((EXPERT_CHUNK, d), lambda i, s: (s, 0)),
                  sel, sel, sel],
        out_specs=sel,
        out_shape=jax.ShapeDtypeStruct((m, n_sel), F32),
        scratch_shapes=[pltpu.VMEM((rows, n_sel), F32)],
        compiler_params=_params("parallel", "arbitrary"),
        name="peer_act",
    )(hn, u, e1, e2, gate)


def _peer_mix_kernel(x_ref, v_ref, e1_ref, e2_ref, coef_ref, o_ref, w_ref):
    s = pl.program_id(1)
    per_step = EXPERT_CHUNK // N_KEYS

    @pl.when(s == 0)
    def _():
        o_ref[...] = x_ref[...]
        n_sel = e1_ref.shape[1]
        key_iota = lax.broadcasted_iota(I32, (N_KEYS, n_sel), 0)

        def row_body(r, carry):
            e1 = e1_ref[pl.ds(r, 1), :]
            e2 = e2_ref[pl.ds(r, 1), :]
            cf = coef_ref[pl.ds(r, 1), :]
            a_t = jnp.where(key_iota == e1, cf, 0.0).astype(BF16)
            b_t = (key_iota == e2).astype(BF16)
            w_ref[pl.ds(pl.multiple_of(r * N_KEYS, N_KEYS), N_KEYS), :] = lax.dot_general(
                a_t, b_t, _NT, preferred_element_type=F32)
            return carry

        lax.fori_loop(0, x_ref.shape[0], row_body, 0, unroll=64)

    rows = x_ref.shape[0]
    tiles = [w_ref[pl.ds(s * per_step + t, rows, stride=N_KEYS), :].astype(BF16) for t in range(per_step)]
    o_ref[...] += jnp.dot(jnp.concatenate(tiles, axis=1), v_ref[...], preferred_element_type=F32)


def peer_mix(x, v, e1, e2, coef):
    m, d = x.shape
    n_sel = e1.shape[1]
    rows = next(r for r in (3 * ROW_TILE // 2, ROW_TILE) if m % r == 0)
    sel = pl.BlockSpec((rows, n_sel), lambda i, s: (i, 0))
    return pl.pallas_call(
        _peer_mix_kernel,
        grid=(m // rows, v.shape[0] // EXPERT_CHUNK),
        in_specs=[pl.BlockSpec((rows, d), lambda i, s: (i, 0)),
                  pl.BlockSpec((EXPERT_CHUNK, d), lambda i, s: (s, 0)),
                  sel, sel, sel],
        out_specs=pl.BlockSpec((rows, d), lambda i, s: (i, 0)),
        out_shape=jax.ShapeDtypeStruct((m, d), F32),
        scratch_shapes=[pltpu.VMEM((rows * N_KEYS, N_KEYS), F32)],
        compiler_params=_params("parallel", "arbitrary"),
        name="peer_mix",
    )(x, v, e1, e2, coef)


def _layer(x_all, n_prompt, n_sample, cache_k, cache_v, page_table, wkv_s0, shift_s0, lw):
    m, d_model = x_all.shape
    d_a = lw["beta_a"].shape[0]
    d_b = lw["w0"].shape[0]
    n_heads_b = d_b // HS_B
    d_shift = lw["mu_shift"].shape[0]
    sample = slice(n_prompt, n_prompt + n_sample)

    hn = rmsnorm_rows(x_all, lw["norm_mix_g"])
    p = matmul_rows([hn], [lw["w_in"].astype(BF16)], tn=1280)
    qn, kn = qk_norm(p, lw["q_norm_g"], lw["k_norm_g"], d_a)
    va = p[:, 2 * d_a:3 * d_a]
    pb = p[:, 3 * d_a:]

    o_a = sb_attention_prompt(qn, kn, p, lw["sb_bias"], lw["beta_a"])
    o_s = sb_attention_sample(qn[sample], cache_k, cache_v, page_table, lw["sb_bias"], lw["beta_a"])
    o_a = lax.dynamic_update_slice(o_a, o_s, (n_prompt, 0))

    r, k, v, kk, beta, ld, g = rwkv_prep(p, 3 * d_a, shift_s0, n_prompt, lw)
    valid = (jnp.arange(m) < n_prompt).astype(F32)[:, None]
    y_b, wkv_p = rwkv_state(*rwkv_chunks(valid, r, k, v, kk, beta, ld), jnp.zeros((n_heads_b, HS_B, HS_B), F32))
    y_s, wkv_s = rwkv_step(*(a[sample] for a in (r, k, v, kk, beta, ld)), wkv_s0)
    y_b = lax.dynamic_update_slice(y_b, y_s, (n_prompt, 0))
    y_b = rwkv_post(y_b, r, k, v, g, lw["ln_x_g"], lw["ln_x_b"], lw["r_k"])

    w_out = lw["w_out"].astype(BF16)
    x1 = matmul_rows([o_a, y_b], [w_out[:d_a], w_out[d_a:]], res=x_all, tn=1024)

    hn2 = rmsnorm_rows(x1, lw["norm_ffn_g"])
    q = matmul_rows([hn2], [lw["peer_wq"].astype(BF16)], tn=1024)
    e1, e2, gate = peer_topk(q, lw["peer_subkeys"])
    coef = peer_act(hn2, lw["peer_u"].astype(BF16), e1, e2, gate)
    x2 = peer_mix(x1, lw["peer_v"].astype(BF16), e1, e2, coef)
    return x2, kn, va, wkv_p[None], wkv_s, pb


_LAYER_WEIGHTS = ("norm_mix_g", "w_in", "q_norm_g", "k_norm_g", "beta_a", "sb_bias", "mu_shift", "w0", "w_up",
                  "a0", "a_up", "g_up", "k_k", "k_a", "r_k", "ln_x_g", "ln_x_b", "w_out", "norm_ffn_g",
                  "peer_wq", "peer_subkeys", "peer_u", "peer_v")


def kernel(x_prompt, x_sample, cache_k, cache_v, page_table, state_wkv, state_shift, meta_tokens, norm_mix_g, w_in, q_norm_g, k_norm_g, beta_a, sb_bias, mu_shift, w0, w_up, a0, a_up, g_up, k_k, k_a, r_k, ln_x_g, ln_x_b, w_out, norm_ffn_g, peer_wq, peer_subkeys, peer_u, peer_v):
    weights = dict(zip(_LAYER_WEIGHTS, (norm_mix_g, w_in, q_norm_g, k_norm_g, beta_a, sb_bias, mu_shift, w0, w_up,
                                        a0, a_up, g_up, k_k, k_a, r_k, ln_x_g, ln_x_b, w_out, norm_ffn_g,
                                        peer_wq, peer_subkeys, peer_u, peer_v)))
    batch, seq, d_model = x_prompt.shape
    assert batch == 1 and x_sample.shape[1] == 1
    n_sample = x_sample.shape[0]
    n_prompt = seq + N_META
    depth = w_in.shape[0]
    m = -(-(n_prompt + n_sample) // ROW_TILE) * ROW_TILE
    x_all = jnp.concatenate([meta_tokens.astype(F32), x_prompt[0], x_sample[:, 0],
                             jnp.zeros((m - n_prompt - n_sample, d_model), F32)], axis=0)
    n_phys = cache_k.shape[1]
    d_a = beta_a.shape[1]
    n_heads_a = d_a // DH_A
    outs = [[] for _ in range(8)]
    for l in range(depth):
        lw = {name: w[l] for name, w in weights.items()}
        x_all, kn, va, wkv_p, wkv_s, pb = _layer(
            x_all, n_prompt, n_sample, cache_k[l].reshape(n_phys * PAGE_SIZE * n_heads_a, DH_A),
            cache_v[l].reshape(n_phys * PAGE_SIZE * n_heads_a, DH_A), page_table, state_wkv[l], state_shift[l], lw)
        heads = lambda a, rows: a[rows].reshape(-1, n_heads_a, DH_A)
        prompt = slice(0, n_prompt)
        sample = slice(n_prompt, n_prompt + n_sample)
        outs[0].append(heads(kn, prompt)[None])
        outs[1].append(heads(va, prompt)[None])
        outs[2].append(heads(kn, sample)[:, None])
        outs[3].append(heads(va, sample)[:, None])
        outs[4].append(wkv_p)
        outs[5].append(wkv_s)
        outs[6].append(pb[n_prompt - 1:n_prompt])
        outs[7].append(pb[sample])
    y_prompt = x_all[N_META:n_prompt][None]
    y_sample = x_all[n_prompt:n_prompt + n_sample][:, None]
    return (y_prompt, y_sample) + tuple(jnp.stack(o) for o in outs)
```
